```python
import math
import jax, jax.numpy as jnp
from jax import lax
import numpy as np

D_MODEL = 1024
BATCH = 2
SEQ = 8192
DEPTH = 1

CHUNK = 64
BLOCK_Q = 128
SB_HEADS = 8
SB_HEAD_DIM = 64
SB_WIDTH = SB_HEADS * SB_HEAD_DIM
DIFF_HEADS = 4
DIFF_HEAD_DIM = 64
DIFF_WIDTH = DIFF_HEADS * 2 * DIFF_HEAD_DIM
REL_BUCKETS = 32
REL_MAX_DIST = 128
D_FF = 4 * D_MODEL
N_BRANCHES = 2
IN_COLS = 3 * SB_WIDTH + 3 * DIFF_WIDTH + N_BRANCHES * D_MODEL
NORM_EPS = 1e-6

kernel_name = "hybrid_stickbreak_diffattn_block"


def rms_norm(x, g):
    xf = x.astype(jnp.float32)
    y = xf * lax.rsqrt(jnp.mean(xf * xf, axis=-1, keepdims=True) + NORM_EPS)
    return (y * g.astype(jnp.float32)).astype(x.dtype)


def t5_bucket(rel):
    half = REL_BUCKETS // 2
    max_exact = half // 2
    ret = jnp.where(rel > 0, half, 0)
    n = jnp.abs(rel)
    nf = jnp.maximum(n, 1).astype(jnp.float32)
    large = max_exact + (jnp.log(nf / max_exact) / math.log(REL_MAX_DIST / max_exact)
                         * (half - max_exact)).astype(jnp.int32)
    large = jnp.minimum(large, half - 1)
    return ret + jnp.where(n < max_exact, n, large)


def to_blocks(t):
    b, s, h, d = t.shape
    return t.reshape(b, s // BLOCK_Q, BLOCK_Q, h, d).transpose(1, 0, 3, 2, 4)


def from_blocks(t):
    nb, b, h, q, d = t.shape
    return t.transpose(1, 0, 3, 2, 4).reshape(b, nb * q, h * d)


def stick_breaking_attention(q, k, v):
    seq = q.shape[1]
    scale = SB_HEAD_DIM ** -0.5
    kh = k.transpose(0, 2, 1, 3)
    vh = v.transpose(0, 2, 1, 3)
    k_pos = jnp.arange(seq, dtype=jnp.int32)
    q_pos = k_pos.reshape(seq // BLOCK_Q, BLOCK_Q)

    def block(args):
        qb, qp = args
        z = jnp.einsum('bhqd,bhkd->bhqk', qb, kh).astype(jnp.float32) * scale
        causal = k_pos[None, :] < qp[:, None]
        log_beta = jax.nn.log_sigmoid(z)
        log_one_minus = jnp.where(causal, jax.nn.log_sigmoid(-z), 0.0)
        shifted = jnp.concatenate([log_one_minus[..., 1:], jnp.zeros_like(log_one_minus[..., :1])], axis=-1)
        tail = lax.cumsum(shifted, axis=shifted.ndim - 1, reverse=True)
        w = jnp.where(causal, jnp.exp(log_beta + tail), 0.0)
        return jnp.einsum('bhqk,bhkd->bhqd', w.astype(vh.dtype), vh)

    out = lax.map(block, (to_blocks(q), q_pos))
    return from_blocks(out)


def differential_attention(q1, q2, k1, k2, v, lam, rel_bias):
    seq = q1.shape[1]
    scale = DIFF_HEAD_DIM ** -0.5
    k1h = k1.transpose(0, 2, 1, 3)
    k2h = k2.transpose(0, 2, 1, 3)
    vh = v.transpose(0, 2, 1, 3)
    k_pos = jnp.arange(seq, dtype=jnp.int32)
    q_pos = k_pos.reshape(seq // BLOCK_Q, BLOCK_Q)
    neg = jnp.finfo(jnp.float32).min
    table = rel_bias.astype(jnp.float32)

    def block(args):
        q1b, q2b, qp = args
        allowed = k_pos[None, :] < (qp[:, None] // CHUNK + 1) * CHUNK
        bias = jnp.transpose(table[t5_bucket(k_pos[None, :] - qp[:, None])], (2, 0, 1))

        def probs(qb, kh):
            s = jnp.einsum('bhqd,bhkd->bhqk', qb, kh).astype(jnp.float32) * scale + bias
            return jax.nn.softmax(jnp.where(allowed, s, neg), axis=-1)

        a = probs(q1b, k1h) - lam * probs(q2b, k2h)
        return jnp.einsum('bhqk,bhkd->bhqd', a.astype(vh.dtype), vh)

    out = lax.map(block, (to_blocks(q1), to_blocks(q2), q_pos))
    nb, b, h, q, d = out.shape
    return out.transpose(1, 0, 3, 2, 4).reshape(b, nb * q, h, d)


def setup_inputs(seed: int = 0) -> dict:
    key = jax.random.key(seed)
    ks = jax.random.split(key, 20)
    f32 = jnp.float32

    def w(k, shape, fan_in):
        return jax.random.normal(k, shape, f32) * (fan_in ** -0.5)

    def gain(k, shape):
        return 1.0 + 0.02 * jax.random.normal(k, shape, f32)

    return {
        "x": jax.random.normal(ks[0], (BATCH, SEQ, D_MODEL), f32),
        "w_in": w(ks[1], (DEPTH, D_MODEL, IN_COLS), D_MODEL),
        "w_sb_out": w(ks[2], (DEPTH, SB_WIDTH, D_MODEL), SB_WIDTH),
        "w_diff_out": w(ks[3], (DEPTH, DIFF_WIDTH, D_MODEL), DIFF_WIDTH),
        "w_o": w(ks[4], (DEPTH, D_MODEL, D_MODEL), D_MODEL),
        "lambda_q1": 0.1 * jax.random.normal(ks[5], (DEPTH, DIFF_HEAD_DIM), f32),
        "lambda_k1": 0.1 * jax.random.normal(ks[6], (DEPTH, DIFF_HEAD_DIM), f32),
        "lambda_q2": 0.1 * jax.random.normal(ks[7], (DEPTH, DIFF_HEAD_DIM), f32),
        "lambda_k2": 0.1 * jax.random.normal(ks[8], (DEPTH, DIFF_HEAD_DIM), f32),
        "w_subln": gain(ks[9], (DEPTH, 2 * DIFF_HEAD_DIM)),
        "rel_bias": 0.5 * jax.random.normal(ks[10], (REL_BUCKETS, DIFF_HEADS), f32),
        "g_pre_mix": gain(ks[11], (DEPTH, D_MODEL)),
        "g_post_mix": gain(ks[12], (DEPTH, D_MODEL)),
        "g_pre_mlp": gain(ks[13], (DEPTH, D_MODEL)),
        "g_post_mlp": gain(ks[14], (DEPTH, D_MODEL)),
        "w_up": w(ks[15], (DEPTH, D_MODEL, D_FF), D_MODEL),
        "w_down": w(ks[16], (DEPTH, D_FF, D_MODEL), D_FF),
    }


def reference(x, w_in, w_sb_out, w_diff_out, w_o, lambda_q1, lambda_k1, lambda_q2, lambda_k2,
              w_subln, rel_bias, g_pre_mix, g_post_mix, g_pre_mlp, g_post_mlp, w_up, w_down):
    b, s, _ = x.shape
    for l in range(DEPTH):
        h = rms_norm(x, g_pre_mix[l])
        proj = h @ w_in[l]
        c0 = 3 * SB_WIDTH
        c1 = c0 + 3 * DIFF_WIDTH
        sb_q, sb_k, sb_v = jnp.split(proj[..., :c0], 3, axis=-1)
        d_q, d_k, d_v = jnp.split(proj[..., c0:c1], 3, axis=-1)
        gates = jax.nn.sigmoid(proj[..., c1:].astype(jnp.float32)).astype(x.dtype)
        gate_sb, gate_diff = jnp.split(gates, N_BRANCHES, axis=-1)

        sb_shape = (b, s, SB_HEADS, SB_HEAD_DIM)
        y_sb = stick_breaking_attention(sb_q.reshape(sb_shape), sb_k.reshape(sb_shape), sb_v.reshape(sb_shape))

        d_q = d_q.reshape(b, s, DIFF_HEADS, 2, DIFF_HEAD_DIM)
        d_k = d_k.reshape(b, s, DIFF_HEADS, 2, DIFF_HEAD_DIM)
        d_v = d_v.reshape(b, s, DIFF_HEADS, 2 * DIFF_HEAD_DIM)
        lam_init = 0.8 - 0.6 * math.exp(-0.3 * l)
        lam = (jnp.exp(jnp.sum(lambda_q1[l].astype(jnp.float32) * lambda_k1[l].astype(jnp.float32)))
               - jnp.exp(jnp.sum(lambda_q2[l].astype(jnp.float32) * lambda_k2[l].astype(jnp.float32)))
               + lam_init)
        y_diff = differential_attention(d_q[..., 0, :], d_q[..., 1, :], d_k[..., 0, :], d_k[..., 1, :],
                                        d_v, lam, rel_bias)
        y_diff = (rms_norm(y_diff, w_subln[l]) * (1.0 - lam_init)).reshape(b, s, DIFF_WIDTH)

        merged = gate_sb * (y_sb @ w_sb_out[l]) + gate_diff * (y_diff @ w_diff_out[l])
        x = x + rms_norm(merged @ w_o[l], g_post_mix[l])

        h = rms_norm(x, g_pre_mlp[l])
        u = jnp.square(jax.nn.relu(h @ w_up[l]))
        x = x + rms_norm(u @ w_down[l], g_post_mlp[l])
    return x
```

```python
import functools
import math

import jax
import jax.numpy as jnp
from jax import lax
from jax.experimental import pallas as pl
from jax.experimental.pallas import tpu as pltpu

F32 = jnp.float32
BF16 = jnp.bfloat16

NORM_EPS = 1e-6
CHUNK = 64
SB_HEADS = 8
HEAD_DIM = 64
DIFF_HEADS = 4
PAIR = 2 * HEAD_DIM
REL_BUCKETS = 32
REL_MAX_DIST = 128
MASKED = -1e30

VMEM_LIMIT = 56 * 1024 * 1024

TQ = 256
TK = 256


def _rms(xf, g):
    ms = jnp.mean(xf * xf, axis=-1, keepdims=True)
    return xf * lax.rsqrt(ms + NORM_EPS) * g


def _dot(a, b):
    return jnp.dot(a, b, preferred_element_type=F32)


def _dot_nt(a, b):
    return lax.dot_general(a, b, (((1,), (1,)), ((), ())), preferred_element_type=F32)


def _proj_kernel(x_ref, g_ref, w_ref, o_ref, h_ref):
    @pl.when(pl.program_id(1) == 0)
    def _():
        h_ref[...] = _rms(x_ref[...], g_ref[...]).astype(BF16)

    o_ref[...] = _dot(h_ref[...], w_ref[...]).astype(BF16)


def _proj(x2, g, w, tm, tn):
    m, d = x2.shape
    n = w.shape[1]
    return pl.pallas_call(
        _proj_kernel,
        grid=(m // tm, n // tn),
        in_specs=[
            pl.BlockSpec((tm, d), lambda i, j: (i, 0)),
            pl.BlockSpec((1, d), lambda i, j: (0, 0)),
            pl.BlockSpec((d, tn), lambda i, j: (0, j)),
        ],
        out_specs=pl.BlockSpec((tm, tn), lambda i, j: (i, j)),
        out_shape=jax.ShapeDtypeStruct((m, n), BF16),
        scratch_shapes=[pltpu.VMEM((tm, d), BF16)],
        compiler_params=pltpu.CompilerParams(
            dimension_semantics=("parallel", "arbitrary"),
            vmem_limit_bytes=VMEM_LIMIT),
    )(x2, g, w)


def _sb_tile(q, k, v, upper, c, acc, mask):
    z = _dot_nt(q, k)
    soft = jnp.log(1.0 + jnp.exp(-jnp.abs(z)))
    log_beta = jnp.minimum(z, 0.0) - soft
    log_om = log_beta - z
    if mask is not None:
        log_om = jnp.where(mask, log_om, 0.0)
    hi = log_om.astype(BF16)
    lo = (log_om - hi.astype(F32)).astype(BF16)
    tail = _dot(hi, upper) + _dot(lo, upper)
    w = jnp.exp(log_beta + tail + c)
    if mask is not None:
        w = jnp.where(mask, w, 0.0)
    acc = acc + _dot(w.astype(BF16), v)
    c = c + jnp.sum(log_om, axis=1, keepdims=True)
    return c, acc


def _sb_kernel(q_ref, k_ref, v_ref, o_ref):
    qi = pl.program_id(2)
    row = lax.broadcasted_iota(jnp.int32, (TQ, TK), 0)
    col = lax.broadcasted_iota(jnp.int32, (TQ, TK), 1)
    causal = col < row
    upper = jnp.where(row > col, 1.0, 0.0).astype(BF16)
    lane = lax.broadcasted_iota(jnp.int32, (TQ, PAIR), 1)
    q_all = q_ref[0]
    accs = []
    for h in range(2):
        in_head = (lane >= h * HEAD_DIM) & (lane < (h + 1) * HEAD_DIM)
        q = jnp.where(in_head, q_all, jnp.zeros_like(q_all))

        def tile(kb, c, acc, mask, q=q):
            start = pl.multiple_of(kb * TK, TK)
            k = k_ref[0, pl.ds(start, TK), :]
            v = v_ref[0, pl.ds(start, TK), :]
            return _sb_tile(q, k, v, upper, c, acc, mask)

        c0 = jnp.zeros((TQ, 1), F32)
        acc0 = jnp.zeros((TQ, PAIR), F32)
        c, acc = tile(qi, c0, acc0, causal)

        def body(j, carry, tile=tile):
            return tile(qi - 1 - j, carry[0], carry[1], None)

        c, acc = lax.fori_loop(0, qi, body, (c, acc))
        accs.append(acc)
    o_ref[0] = jnp.where(lane < HEAD_DIM, accs[0], accs[1]).astype(BF16)


def _sb_attention(qkv, b, s):
    n_pairs = SB_HEADS // 2
    return pl.pallas_call(
        _sb_kernel,
        grid=(b, n_pairs, s // TQ),
        in_specs=[
            pl.BlockSpec((1, TQ, PAIR), lambda bi, p, qi: (bi, qi, p)),
            pl.BlockSpec((1, s, PAIR), lambda bi, p, qi: (bi, 0, n_pairs + p)),
            pl.BlockSpec((1, s, PAIR), lambda bi, p, qi: (bi, 0, 2 * n_pairs + p)),
        ],
        out_specs=pl.BlockSpec((1, TQ, PAIR), lambda bi, p, qi: (bi, qi, p)),
        out_shape=jax.ShapeDtypeStruct((b, s, SB_HEADS * HEAD_DIM), BF16),
        compiler_params=pltpu.CompilerParams(
            dimension_semantics=("parallel", "parallel", "arbitrary"),
            vmem_limit_bytes=VMEM_LIMIT),
    )(qkv, qkv, qkv)


def _softmax_tile(q, k, v, bias, state, mask):
    m, l, acc = state
    s = _dot_nt(q, k) + bias
    if mask is not None:
        s = jnp.where(mask, s, MASKED)
    m_new = jnp.maximum(m, jnp.max(s, axis=1, keepdims=True))
    alpha = jnp.exp(m - m_new)
    p = jnp.exp(s - m_new)
    l = alpha * l + jnp.sum(p, axis=1, keepdims=True)
    acc = alpha * acc + _dot(p.astype(BF16), v)
    return m_new, l, acc


def _diff_kernel(q_ref, k_ref, v_ref, bias_ref, lq1_ref, lk1_ref, lq2_ref, lk2_ref, g_ref,
                 o_ref, *, lam_init):
    qi = pl.program_id(2)
    row = lax.broadcasted_iota(jnp.int32, (TQ, TK), 0)
    col = lax.broadcasted_iota(jnp.int32, (TQ, TK), 1)
    allowed = col < (row // CHUNK + 1) * CHUNK
    lane = lax.broadcasted_iota(jnp.int32, (TQ, PAIR), 1)
    q_all = q_ref[0]
    far_bias = bias_ref[0, 1, TQ - 1:TQ, 0:1]

    outs = []
    for half in range(2):
        in_half = (lane >= half * HEAD_DIM) & (lane < (half + 1) * HEAD_DIM)
        q = jnp.where(in_half, q_all, jnp.zeros_like(q_all))

        def tile(kb, bias, state, mask, q=q):
            start = pl.multiple_of(kb * TK, TK)
            k = k_ref[0, pl.ds(start, TK), :]
            v = v_ref[0, pl.ds(start, TK), :]
            return _softmax_tile(q, k, v, bias, state, mask)

        state = (jnp.full((TQ, 1), MASKED, F32), jnp.zeros((TQ, 1), F32),
                 jnp.zeros((TQ, PAIR), F32))
        state = tile(qi, bias_ref[0, 0], state, allowed)
        state = lax.cond(
            qi >= 1,
            lambda st, tile=tile: tile(qi - 1, bias_ref[0, 1], st, None),
            lambda st: st,
            state)

        def body(j, st, tile=tile):
            return tile(qi - 2 - j, far_bias, st, None)

        state = lax.fori_loop(0, jnp.maximum(qi - 1, 0), body, state)
        _, l, acc = state
        outs.append(acc * (1.0 / l))

    lam = (jnp.exp(jnp.sum(lq1_ref[...] * lk1_ref[...], axis=1, keepdims=True))
           - jnp.exp(jnp.sum(lq2_ref[...] * lk2_ref[...], axis=1, keepdims=True))
           + lam_init)
    y = outs[0] - lam * outs[1]
    o_ref[0] = (_rms(y, g_ref[...]) * (1.0 - lam_init)).astype(BF16)


def _diff_attention(qkv, bias_tiles, lq1, lk1, lq2, lk2, g_subln, b, s, lam_init):
    first = 3 * (SB_HEADS // 2)
    vec = pl.BlockSpec((1, HEAD_DIM), lambda bi, h, qi: (0, 0))
    return pl.pallas_call(
        functools.partial(_diff_kernel, lam_init=lam_init),
        grid=(b, DIFF_HEADS, s // TQ),
        in_specs=[
            pl.BlockSpec((1, TQ, PAIR), lambda bi, h, qi: (bi, qi, first + h)),
            pl.BlockSpec((1, s, PAIR), lambda bi, h, qi: (bi, 0, first + DIFF_HEADS + h)),
            pl.BlockSpec((1, s, PAIR), lambda bi, h, qi: (bi, 0, first + 2 * DIFF_HEADS + h)),
            pl.BlockSpec((1, 2, TQ, TK), lambda bi, h, qi: (h, 0, 0, 0)),
            vec, vec, vec, vec,
            pl.BlockSpec((1, PAIR), lambda bi, h, qi: (0, 0)),
        ],
        out_specs=pl.BlockSpec((1, TQ, PAIR), lambda bi, h, qi: (bi, qi, h)),
        out_shape=jax.ShapeDtypeStruct((b, s, DIFF_HEADS * PAIR), BF16),
        compiler_params=pltpu.CompilerParams(
            dimension_semantics=("parallel", "parallel", "arbitrary"),
            vmem_limit_bytes=VMEM_LIMIT),
    )(qkv, qkv, qkv, bias_tiles, lq1, lk1, lq2, lk2, g_subln)


def _t5_bucket(rel):
    half = REL_BUCKETS // 2
    max_exact = half // 2
    ret = jnp.where(rel > 0, half, 0)
    n = jnp.abs(rel)
    nf = jnp.maximum(n, 1).astype(F32)
    large = max_exact + (jnp.log(nf / max_exact) / math.log(REL_MAX_DIST / max_exact)
                         * (half - max_exact)).astype(jnp.int32)
    large = jnp.minimum(large, half - 1)
    return ret + jnp.where(n < max_exact, n, large)


def _bias_tiles(rel_bias):
    rel = jnp.arange(-2 * TQ + 1, TK, dtype=jnp.int32)
    by_rel = rel_bias.astype(F32)[_t5_bucket(rel)].T
    i = jnp.arange(TQ)[:, None]
    j = jnp.arange(TK)[None, :]
    diag = j - i + (2 * TQ - 1)
    older = diag - TK
    return jnp.stack([by_rel[:, diag], by_rel[:, older]], axis=1)


def _merge_kernel(x_ref, ysb_ref, ydf_ref, gpre_ref, wg_ref, wsb_ref, wdf_ref, wo_ref,
                  gpost_ref, o_ref):
    d = x_ref.shape[1]
    x = x_ref[...]
    h = _rms(x, gpre_ref[...]).astype(BF16)
    gates = jax.nn.sigmoid(_dot(h, wg_ref[...]))
    merged = (gates[:, :d] * _dot(ysb_ref[...], wsb_ref[...])
              + gates[:, d:] * _dot(ydf_ref[...], wdf_ref[...]))
    o = _dot(merged.astype(BF16), wo_ref[...])
    o_ref[...] = x + _rms(o, gpost_ref[...])


def _merge(x2, ysb, ydf, gpre, wg, wsb, wdf, wo, gpost, tm):
    m, d = x2.shape

    def full(a):
        return pl.BlockSpec(a.shape, lambda i: (0, 0))

    def rows(a):
        return pl.BlockSpec((tm, a.shape[1]), lambda i: (i, 0))

    return pl.pallas_call(
        _merge_kernel,
        grid=(m // tm,),
        in_specs=[rows(x2), rows(ysb), rows(ydf), full(gpre), full(wg), full(wsb), full(wdf),
                  full(wo), full(gpost)],
        out_specs=pl.BlockSpec((tm, d), lambda i: (i, 0)),
        out_shape=jax.ShapeDtypeStruct((m, d), F32),
        compiler_params=pltpu.CompilerParams(
            dimension_semantics=("parallel",),
            vmem_limit_bytes=VMEM_LIMIT),
    )(x2, ysb, ydf, gpre, wg, wsb, wdf, wo, gpost)


def _mlp_kernel(x_ref, gpre_ref, wup_ref, wdn_ref, gpost_ref, o_ref, h_ref, acc_ref):
    f = pl.program_id(1)

    @pl.when(f == 0)
    def _():
        h_ref[...] = _rms(x_ref[...], gpre_ref[...]).astype(BF16)
        acc_ref[...] = jnp.zeros_like(acc_ref)

    u = jnp.square(jnp.maximum(_dot(h_ref[...], wup_ref[...]), 0.0))
    acc_ref[...] += _dot(u.astype(BF16), wdn_ref[...])

    @pl.when(f == pl.num_programs(1) - 1)
    def _():
        o_ref[...] = x_ref[...] + _rms(acc_ref[...], gpost_ref[...])


def _mlp(x2, gpre, wup, wdn, gpost, tm, tf):
    m, d = x2.shape
    ff = wup.shape[1]
    return pl.pallas_call(
        _mlp_kernel,
        grid=(m // tm, ff // tf),
        in_specs=[
            pl.BlockSpec((tm, d), lambda i, f: (i, 0)),
            pl.BlockSpec((1, d), lambda i, f: (0, 0)),
            pl.BlockSpec((d, tf), lambda i, f: (0, f)),
            pl.BlockSpec((tf, d), lambda i, f: (f, 0)),
            pl.BlockSpec((1, d), lambda i, f: (0, 0)),
        ],
        out_specs=pl.BlockSpec((tm, d), lambda i, f: (i, 0)),
        out_shape=jax.ShapeDtypeStruct((m, d), F32),
        scratch_shapes=[pltpu.VMEM((tm, d), BF16), pltpu.VMEM((tm, d), F32)],
        compiler_params=pltpu.CompilerParams(
            dimension_semantics=("parallel", "arbitrary"),
            vmem_limit_bytes=VMEM_LIMIT),
    )(x2, gpre, wup, wdn, gpost)


def kernel(x, w_in, w_sb_out, w_diff_out, w_o, lambda_q1, lambda_k1, lambda_q2, lambda_k2, w_subln, rel_bias, g_pre_mix, g_post_mix, g_pre_mlp, g_post_mlp, w_up, w_down):
    b, s, d = x.shape
    depth = w_in.shape[0]
    sb_width = SB_HEADS * HEAD_DIM
    diff_width = DIFF_HEADS * PAIR
    n_qkv = 3 * sb_width + 3 * diff_width
    scale = HEAD_DIM ** -0.5
    col_scale = jnp.ones((n_qkv,), F32)
    col_scale = col_scale.at[:sb_width].set(scale)
    col_scale = col_scale.at[3 * sb_width:3 * sb_width + diff_width].set(scale)
    bias_tiles = _bias_tiles(rel_bias)

    x2 = x.reshape(b * s, d)
    for l in range(depth):
        lam_init = 0.8 - 0.6 * math.exp(-0.3 * l)
        w_qkv = (w_in[l][:, :n_qkv] * col_scale).astype(BF16)
        w_gate = w_in[l][:, n_qkv:].astype(BF16)
        qkv = _proj(x2, g_pre_mix[l][None], w_qkv, 512, 1024).reshape(b, s, n_qkv)
        y_sb = _sb_attention(qkv, b, s)
        y_diff = _diff_attention(qkv, bias_tiles, lambda_q1[l][None], lambda_k1[l][None],
                                 lambda_q2[l][None], lambda_k2[l][None], w_subln[l][None],
                                 b, s, lam_init)
        x2 = _merge(x2, y_sb.reshape(b * s, sb_width), y_diff.reshape(b * s, diff_width),
                    g_pre_mix[l][None], w_gate, w_sb_out[l].astype(BF16),
                    w_diff_out[l].astype(BF16), w_o[l].astype(BF16), g_post_mix[l][None], 256)
        x2 = _mlp(x2, g_pre_mlp[l][None], w_up[l].astype(BF16), w_down[l].astype(BF16),
                  g_post_mlp[l][None], 512, 1024)
    return x2.reshape(b, s, d)
```

```python
import functools
import math

import jax
import jax.numpy as jnp
from jax import lax
from jax.experimental import pallas as pl
from jax.experimental.pallas import tpu as pltpu

F32 = jnp.float32
BF16 = jnp.bfloat16

NORM_EPS = 1e-6
CHUNK = 64
SB_HEADS = 8
HEAD_DIM = 64
DIFF_HEADS = 4
LANES = 128
REL_BUCKETS = 32
REL_MAX_DIST = 128
MASKED = -1e30
LOG2E = 1.4426950408889634

VMEM_LIMIT = 56 * 1024 * 1024

TQ = 256
TK = 256
GROUPS = 2


def _rms(xf, g):
    ms = jnp.mean(xf * xf, axis=-1, keepdims=True)
    return xf * lax.rsqrt(ms + NORM_EPS) * g


def _dot(a, b):
    return jnp.dot(a, b, preferred_element_type=F32)


def _dot_nt(a, b):
    return lax.dot_general(a, b, (((1,), (1,)), ((), ())), preferred_element_type=F32)


def _neg_abs(x):
    bits = lax.bitcast_convert_type(x, jnp.uint32) | jnp.uint32(0x80000000)
    return lax.bitcast_convert_type(bits, F32)


def _half_lane_queries(q_ref):
    lane = lax.broadcasted_iota(jnp.int32, (TQ, LANES), 1)
    qs = []
    for g in range(GROUPS):
        q_all = q_ref[0, :, g * LANES:(g + 1) * LANES]
        for half in range(2):
            keep = (lane >= half * HEAD_DIM) & (lane < (half + 1) * HEAD_DIM)
            qs.append(jnp.where(keep, q_all, jnp.zeros_like(q_all)))
    return qs


def _key_tile(ref, kb, g):
    start = pl.multiple_of(kb * TK, TK)
    return ref[0, pl.ds(start, TK), g * LANES:(g + 1) * LANES]


def _proj_kernel(x_ref, g_ref, w_ref, o_ref, h_ref):
    @pl.when(pl.program_id(1) == 0)
    def _():
        h_ref[...] = _rms(x_ref[...], g_ref[...]).astype(BF16)

    o_ref[...] = _dot(h_ref[...], w_ref[...]).astype(BF16)


def _proj(x2, g, w, tm, tn):
    m, d = x2.shape
    n = w.shape[1]
    return pl.pallas_call(
        _proj_kernel,
        grid=(m // tm, n // tn),
        in_specs=[
            pl.BlockSpec((tm, d), lambda i, j: (i, 0)),
            pl.BlockSpec((1, d), lambda i, j: (0, 0)),
            pl.BlockSpec((d, tn), lambda i, j: (0, j)),
        ],
        out_specs=pl.BlockSpec((tm, tn), lambda i, j: (i, j)),
        out_shape=jax.ShapeDtypeStruct((m, n), BF16),
        scratch_shapes=[pltpu.VMEM((tm, d), BF16)],
        compiler_params=pltpu.CompilerParams(
            dimension_semantics=("parallel", "arbitrary"),
            vmem_limit_bytes=VMEM_LIMIT),
        name="proj",
    )(x2, g, w)


def _sb_tiles(qs, ks, vs, upper, states, mask):
    n = len(qs)
    zs = [_dot_nt(qs[i], ks[i]) * LOG2E for i in range(n)]
    log_betas, log_oms, his, los = [], [], [], []
    for z in zs:
        soft = jnp.log2(1.0 + jnp.exp2(_neg_abs(z)))
        log_beta = jnp.minimum(z, 0.0) - soft
        log_om = log_beta - z
        if mask is not None:
            log_om = jnp.where(mask, log_om, 0.0)
        hi = log_om.astype(BF16)
        log_betas.append(log_beta)
        log_oms.append(log_om)
        his.append(hi)
        los.append((log_om - hi.astype(F32)).astype(BF16))
    tails = [_dot(his[i], upper) + _dot(los[i], upper) for i in range(n)]
    ws = []
    for i in range(n):
        w = jnp.exp2(log_betas[i] + tails[i])
        if mask is not None:
            w = jnp.where(mask, w, 0.0)
        ws.append(w.astype(BF16))
    new = []
    for i in range(n):
        c, acc = states[i]
        acc = acc + jnp.exp2(c) * _dot(ws[i], vs[i])
        c = c + jnp.sum(log_oms[i], axis=1, keepdims=True)
        new.append((c, acc))
    return tuple(new)


def _sb_kernel(q_ref, k_ref, v_ref, o_ref):
    qi = pl.program_id(2)
    row = lax.broadcasted_iota(jnp.int32, (TQ, TK), 0)
    col = lax.broadcasted_iota(jnp.int32, (TQ, TK), 1)
    causal = col < row
    upper = jnp.where(row > col, 1.0, 0.0).astype(BF16)
    qs = _half_lane_queries(q_ref)

    def tiles(kb, states, mask):
        ks = [_key_tile(k_ref, kb, i // 2) for i in range(2 * GROUPS)]
        vs = [_key_tile(v_ref, kb, i // 2) for i in range(2 * GROUPS)]
        return _sb_tiles(qs, ks, vs, upper, states, mask)

    init = tuple((jnp.zeros((TQ, 1), F32), jnp.zeros((TQ, LANES), F32))
                 for _ in range(2 * GROUPS))
    states = tiles(qi, init, causal)
    states = lax.fori_loop(0, qi, lambda j, st: tiles(qi - 1 - j, st, None), states)

    lane = lax.broadcasted_iota(jnp.int32, (TQ, LANES), 1)
    for g in range(GROUPS):
        out = jnp.where(lane < HEAD_DIM, states[2 * g][1], states[2 * g + 1][1])
        o_ref[0, :, g * LANES:(g + 1) * LANES] = out.astype(BF16)


def _sb_attention(qkv, b, s):
    n_blk = SB_HEADS * HEAD_DIM // (GROUPS * LANES)
    width = GROUPS * LANES
    return pl.pallas_call(
        _sb_kernel,
        grid=(b, n_blk, s // TQ),
        in_specs=[
            pl.BlockSpec((1, TQ, width), lambda bi, p, qi: (bi, qi, p)),
            pl.BlockSpec((1, s, width), lambda bi, p, qi: (bi, 0, n_blk + p)),
            pl.BlockSpec((1, s, width), lambda bi, p, qi: (bi, 0, 2 * n_blk + p)),
        ],
        out_specs=pl.BlockSpec((1, TQ, width), lambda bi, p, qi: (bi, qi, p)),
        out_shape=jax.ShapeDtypeStruct((b, s, SB_HEADS * HEAD_DIM), BF16),
        compiler_params=pltpu.CompilerParams(
            dimension_semantics=("parallel", "parallel", "arbitrary"),
            vmem_limit_bytes=VMEM_LIMIT),
        name="sb_attention",
    )(qkv, qkv, qkv)


def _softmax_tiles(qs, ks, vs, biases, states, mask):
    n = len(qs)
    ss = []
    for i in range(n):
        s = _dot_nt(qs[i], ks[i]) + biases[i]
        if mask is not None:
            s = jnp.where(mask, s, MASKED)
        ss.append(s)
    ps, alphas, new = [], [], []
    for i in range(n):
        m, l, _ = states[i]
        m_new = jnp.maximum(m, jnp.max(ss[i], axis=1, keepdims=True))
        alpha = jnp.exp(m - m_new)
        p = jnp.exp(ss[i] - m_new)
        new.append([m_new, alpha * l + jnp.sum(p, axis=1, keepdims=True)])
        alphas.append(alpha)
        ps.append(p.astype(BF16))
    for i in range(n):
        new[i].append(alphas[i] * states[i][2] + _dot(ps[i], vs[i]))
    return tuple(tuple(st) for st in new)


def _diff_kernel(q_ref, k_ref, v_ref, bias_ref, lq1_ref, lk1_ref, lq2_ref, lk2_ref, g_ref,
                 o_ref, *, lam_init):
    qi = pl.program_id(2)
    row = lax.broadcasted_iota(jnp.int32, (TQ, TK), 0)
    col = lax.broadcasted_iota(jnp.int32, (TQ, TK), 1)
    allowed = col < (row // CHUNK + 1) * CHUNK
    qs = _half_lane_queries(q_ref)

    def tiles(kb, states, which, mask):
        ks = [_key_tile(k_ref, kb, i // 2) for i in range(2 * GROUPS)]
        vs = [_key_tile(v_ref, kb, i // 2) for i in range(2 * GROUPS)]
        if which is None:
            biases = [bias_ref[i // 2, 1, TQ - 1:TQ, 0:1] for i in range(2 * GROUPS)]
        else:
            biases = [bias_ref[i // 2, which] for i in range(2 * GROUPS)]
        return _softmax_tiles(qs, ks, vs, biases, states, mask)

    init = tuple((jnp.full((TQ, 1), MASKED, F32), jnp.zeros((TQ, 1), F32),
                  jnp.zeros((TQ, LANES), F32)) for _ in range(2 * GROUPS))
    states = tiles(qi, init, 0, allowed)
    states = lax.cond(qi >= 1, lambda st: tiles(qi - 1, st, 1, None), lambda st: st, states)
    states = lax.fori_loop(0, jnp.maximum(qi - 1, 0),
                           lambda j, st: tiles(qi - 2 - j, st, None, None), states)

    lam = (jnp.exp(jnp.sum(lq1_ref[...] * lk1_ref[...], axis=1, keepdims=True))
           - jnp.exp(jnp.sum(lq2_ref[...] * lk2_ref[...], axis=1, keepdims=True))
           + lam_init)
    for g in range(GROUPS):
        (_, l1, acc1), (_, l2, acc2) = states[2 * g], states[2 * g + 1]
        y = acc1 * (1.0 / l1) - lam * (acc2 * (1.0 / l2))
        o_ref[0, :, g * LANES:(g + 1) * LANES] = (
            _rms(y, g_ref[...]) * (1.0 - lam_init)).astype(BF16)


def _diff_attention(qkv, bias_tiles, lq1, lk1, lq2, lk2, g_subln, b, s, lam_init):
    n_blk = DIFF_HEADS // GROUPS
    first = 3 * SB_HEADS * HEAD_DIM // (GROUPS * LANES)
    width = GROUPS * LANES
    vec = pl.BlockSpec((1, HEAD_DIM), lambda bi, h, qi: (0, 0))
    return pl.pallas_call(
        functools.partial(_diff_kernel, lam_init=lam_init),
        grid=(b, n_blk, s // TQ),
        in_specs=[
            pl.BlockSpec((1, TQ, width), lambda bi, h, qi: (bi, qi, first + h)),
            pl.BlockSpec((1, s, width), lambda bi, h, qi: (bi, 0, first + n_blk + h)),
            pl.BlockSpec((1, s, width), lambda bi, h, qi: (bi, 0, first + 2 * n_blk + h)),
            pl.BlockSpec((GROUPS, 2, TQ, TK), lambda bi, h, qi: (h, 0, 0, 0)),
            vec, vec, vec, vec,
            pl.BlockSpec((1, LANES), lambda bi, h, qi: (0, 0)),
        ],
        out_specs=pl.BlockSpec((1, TQ, width), lambda bi, h, qi: (bi, qi, h)),
        out_shape=jax.ShapeDtypeStruct((b, s, DIFF_HEADS * LANES), BF16),
        compiler_params=pltpu.CompilerParams(
            dimension_semantics=("parallel", "parallel", "arbitrary"),
            vmem_limit_bytes=VMEM_LIMIT),
        name="diff_attention",
    )(qkv, qkv, qkv, bias_tiles, lq1, lk1, lq2, lk2, g_subln)


def _t5_bucket(rel):
    half = REL_BUCKETS // 2
    max_exact = half // 2
    ret = jnp.where(rel > 0, half, 0)
    n = jnp.abs(rel)
    nf = jnp.maximum(n, 1).astype(F32)
    large = max_exact + (jnp.log(nf / max_exact) / math.log(REL_MAX_DIST / max_exact)
                         * (half - max_exact)).astype(jnp.int32)
    large = jnp.minimum(large, half - 1)
    return ret + jnp.where(n < max_exact, n, large)


def _toeplitz(vals, first):
    period = vals.shape[1]
    rolled = jnp.roll(vals, -first, axis=1)
    skew = jnp.tile(rolled, (1, TQ))[:, :TQ * (period - 1)].reshape(-1, TQ, period - 1)
    return skew[:, :, :TK]


def _bias_tiles(rel_bias):
    rel = jnp.arange(-2 * TQ + 1, TK + 1, dtype=jnp.int32)
    by_rel = rel_bias.astype(F32)[_t5_bucket(rel)].T
    diag = _toeplitz(by_rel, 2 * TQ - 1)
    older = _toeplitz(by_rel, 2 * TQ - 1 - TK)
    return jnp.stack([diag, older], axis=1)


def _merge_kernel(x_ref, ysb_ref, ydf_ref, gpre_ref, wg_ref, wsb_ref, wdf_ref, wo_ref,
                  gpost_ref, o_ref):
    d = x_ref.shape[1]
    x = x_ref[...]
    h = _rms(x, gpre_ref[...]).astype(BF16)
    gates = jax.nn.sigmoid(_dot(h, wg_ref[...]))
    merged = (gates[:, :d] * _dot(ysb_ref[...], wsb_ref[...])
              + gates[:, d:] * _dot(ydf_ref[...], wdf_ref[...]))
    o = _dot(merged.astype(BF16), wo_ref[...])
    o_ref[...] = x + _rms(o, gpost_ref[...])


def _merge(x2, ysb, ydf, gpre, wg, wsb, wdf, wo, gpost, tm):
    m, d = x2.shape

    def full(a):
        return pl.BlockSpec(a.shape, lambda i: (0, 0))

    def rows(a):
        return pl.BlockSpec((tm, a.shape[1]), lambda i: (i, 0))

    return pl.pallas_call(
        _merge_kernel,
        grid=(m // tm,),
        in_specs=[rows(x2), rows(ysb), rows(ydf), full(gpre), full(wg), full(wsb), full(wdf),
                  full(wo), full(gpost)],
        out_specs=pl.BlockSpec((tm, d), lambda i: (i, 0)),
        out_shape=jax.ShapeDtypeStruct((m, d), F32),
        compiler_params=pltpu.CompilerParams(
            dimension_semantics=("parallel",),
            vmem_limit_bytes=VMEM_LIMIT),
        name="merge",
    )(x2, ysb, ydf, gpre, wg, wsb, wdf, wo, gpost)


def _mlp_kernel(x_ref, gpre_ref, wup_ref, wdn_ref, gpost_ref, o_ref, h_ref, acc_ref):
    f = pl.program_id(1)

    @pl.when(f == 0)
    def _():
        h_ref[...] = _rms(x_ref[...], gpre_ref[...]).astype(BF16)
        acc_ref[...] = jnp.zeros_like(acc_ref)

    u = jnp.square(jnp.maximum(_dot(h_ref[...], wup_ref[...]), 0.0))
    acc_ref[...] += _dot(u.astype(BF16), wdn_ref[...])

    @pl.when(f == pl.num_programs(1) - 1)
    def _():
        o_ref[...] = x_ref[...] + _rms(acc_ref[...], gpost_ref[...])


def _mlp(x2, gpre, wup, wdn, gpost, tm, tf):
    m, d = x2.shape
    ff = wup.shape[1]
    return pl.pallas_call(
        _mlp_kernel,
        grid=(m // tm, ff // tf),
        in_specs=[
            pl.BlockSpec((tm, d), lambda i, f: (i, 0)),
            pl.BlockSpec((1, d), lambda i, f: (0, 0)),
            pl.BlockSpec((d, tf), lambda i, f: (0, f)),
            pl.BlockSpec((tf, d), lambda i, f: (f, 0)),
            pl.BlockSpec((1, d), lambda i, f: (0, 0)),
        ],
        out_specs=pl.BlockSpec((tm, d), lambda i, f: (i, 0)),
        out_shape=jax.ShapeDtypeStruct((m, d), F32),
        scratch_shapes=[pltpu.VMEM((tm, d), BF16), pltpu.VMEM((tm, d), F32)],
        compiler_params=pltpu.CompilerParams(
            dimension_semantics=("parallel", "arbitrary"),
            vmem_limit_bytes=VMEM_LIMIT),
        name="mlp",
    )(x2, gpre, wup, wdn, gpost)


def kernel(x, w_in, w_sb_out, w_diff_out, w_o, lambda_q1, lambda_k1, lambda_q2, lambda_k2, w_subln, rel_bias, g_pre_mix, g_post_mix, g_pre_mlp, g_post_mlp, w_up, w_down):
    b, s, d = x.shape
    depth = w_in.shape[0]
    sb_width = SB_HEADS * HEAD_DIM
    diff_width = DIFF_HEADS * LANES
    n_qkv = 3 * sb_width + 3 * diff_width
    scale = HEAD_DIM ** -0.5
    col_scale = jnp.ones((n_qkv,), F32)
    col_scale = col_scale.at[:sb_width].set(scale)
    col_scale = col_scale.at[3 * sb_width:3 * sb_width + diff_width].set(scale)
    bias_tiles = _bias_tiles(rel_bias)

    x2 = x.reshape(b * s, d)
    for l in range(depth):
        lam_init = 0.8 - 0.6 * math.exp(-0.3 * l)
        w_qkv = (w_in[l][:, :n_qkv] * col_scale).astype(BF16)
        w_gate = w_in[l][:, n_qkv:].astype(BF16)
        qkv = _proj(x2, g_pre_mix[l][None], w_qkv, 512, 1024).reshape(b, s, n_qkv)
        y_sb = _sb_attention(qkv, b, s)
        y_diff = _diff_attention(qkv, bias_tiles, lambda_q1[l][None], lambda_k1[l][None],
                                 lambda_q2[l][None], lambda_k2[l][None], w_subln[l][None],
                                 b, s, lam_init)
        x2 = _merge(x2, y_sb.reshape(b * s, sb_width), y_diff.reshape(b * s, diff_width),
                    g_pre_mix[l][None], w_gate, w_sb_out[l].astype(BF16),
                    w_diff_out[l].astype(BF16), w_o[l].astype(BF16), g_post_mix[l][None], 256)
        x2 = _mlp(x2, g_pre_mlp[l][None], w_up[l].astype(BF16), w_down[l].astype(BF16),
                  g_post_mlp[l][None], 512, 1024)
    return x2.reshape(b, s, d)
```

```python
import functools
import math

import jax
import jax.numpy as jnp
from jax import lax
from jax.experimental import pallas as pl
from jax.experimental.pallas import tpu as pltpu

F32 = jnp.float32
BF16 = jnp.bfloat16

NORM_EPS = 1e-6
CHUNK = 64
SB_HEADS = 8
HEAD_DIM = 64
DIFF_HEADS = 4
LANES = 128
REL_BUCKETS = 32
REL_MAX_DIST = 128
MASKED = -1e30
LOG2E = 1.4426950408889634
DEAD_LOG2 = -160.0

VMEM_LIMIT = 56 * 1024 * 1024

TQ = 256
TK = 256
GROUPS = 2


def _rms(xf, g):
    ms = jnp.mean(xf * xf, axis=-1, keepdims=True)
    return xf * lax.rsqrt(ms + NORM_EPS) * g


def _dot(a, b):
    return jnp.dot(a, b, preferred_element_type=F32)


def _dot_nt(a, b):
    return lax.dot_general(a, b, (((1,), (1,)), ((), ())), preferred_element_type=F32)


def _neg_abs(x):
    bits = lax.bitcast_convert_type(x, jnp.uint32) | jnp.uint32(0x80000000)
    return lax.bitcast_convert_type(bits, F32)


def _half_lane_queries(q_ref):
    lane = lax.broadcasted_iota(jnp.int32, (TQ, LANES), 1)
    qs = []
    for g in range(GROUPS):
        q_all = q_ref[0, :, g * LANES:(g + 1) * LANES]
        for half in range(2):
            keep = (lane >= half * HEAD_DIM) & (lane < (half + 1) * HEAD_DIM)
            qs.append(jnp.where(keep, q_all, jnp.zeros_like(q_all)))
    return qs


def _key_tile(ref, kb, g):
    start = pl.multiple_of(kb * TK, TK)
    return ref[0, pl.ds(start, TK), g * LANES:(g + 1) * LANES]


def _proj_kernel(x_ref, g_ref, w_ref, o_ref, h_ref):
    @pl.when(pl.program_id(1) == 0)
    def _():
        h_ref[...] = _rms(x_ref[...], g_ref[...]).astype(BF16)

    o_ref[...] = _dot(h_ref[...], w_ref[...]).astype(BF16)


def _proj(x2, g, w, tm, tn):
    m, d = x2.shape
    n = w.shape[1]
    return pl.pallas_call(
        _proj_kernel,
        grid=(m // tm, n // tn),
        in_specs=[
            pl.BlockSpec((tm, d), lambda i, j: (i, 0)),
            pl.BlockSpec((1, d), lambda i, j: (0, 0)),
            pl.BlockSpec((d, tn), lambda i, j: (0, j)),
        ],
        out_specs=pl.BlockSpec((tm, tn), lambda i, j: (i, j)),
        out_shape=jax.ShapeDtypeStruct((m, n), BF16),
        scratch_shapes=[pltpu.VMEM((tm, d), BF16)],
        compiler_params=pltpu.CompilerParams(
            dimension_semantics=("parallel", "arbitrary"),
            vmem_limit_bytes=VMEM_LIMIT),
        name="proj",
    )(x2, g, w)


def _sb_tiles(qs, ks, vs, upper, states, mask):
    n = len(qs)
    zs = [_dot_nt(qs[i], ks[i]) * LOG2E for i in range(n)]
    log_betas, log_oms, his, los = [], [], [], []
    for z in zs:
        soft = jnp.log2(1.0 + jnp.exp2(_neg_abs(z)))
        log_beta = jnp.minimum(z, 0.0) - soft
        log_om = log_beta - z
        if mask is not None:
            log_om = jnp.where(mask, log_om, 0.0)
        hi = log_om.astype(BF16)
        log_betas.append(log_beta)
        log_oms.append(log_om)
        his.append(hi)
        los.append((log_om - hi.astype(F32)).astype(BF16))
    tails = [_dot(his[i], upper) + _dot(los[i], upper) for i in range(n)]
    ws = []
    for i in range(n):
        w = jnp.exp2(log_betas[i] + tails[i])
        if mask is not None:
            w = jnp.where(mask, w, 0.0)
        ws.append(w.astype(BF16))
    new = []
    for i in range(n):
        c, acc = states[i]
        acc = acc + jnp.exp2(c) * _dot(ws[i], vs[i])
        c = c + jnp.sum(log_oms[i], axis=1, keepdims=True)
        new.append((c, acc))
    return tuple(new)


def _sb_kernel(q_ref, k_ref, v_ref, o_ref):
    qi = pl.program_id(2)
    row = lax.broadcasted_iota(jnp.int32, (TQ, TK), 0)
    col = lax.broadcasted_iota(jnp.int32, (TQ, TK), 1)
    causal = col < row
    upper = jnp.where(row > col, 1.0, 0.0).astype(BF16)
    qs = _half_lane_queries(q_ref)

    def tiles(kb, states, mask):
        ks = [_key_tile(k_ref, kb, i // 2) for i in range(2 * GROUPS)]
        vs = [_key_tile(v_ref, kb, i // 2) for i in range(2 * GROUPS)]
        return _sb_tiles(qs, ks, vs, upper, states, mask)

    init = tuple((jnp.zeros((TQ, 1), F32), jnp.zeros((TQ, LANES), F32))
                 for _ in range(2 * GROUPS))
    def any_weight_left(states):
        c_max = functools.reduce(jnp.maximum, [jnp.max(st[0]) for st in states])
        return c_max > DEAD_LOG2

    def step(carry):
        j, _, states = carry
        states = tiles(qi - 1 - j, states, None)
        return j + 1, any_weight_left(states), states

    states = tiles(qi, init, causal)
    _, _, states = lax.while_loop(lambda carry: (carry[0] < qi) & carry[1], step,
                                  (jnp.int32(0), any_weight_left(states), states))

    lane = lax.broadcasted_iota(jnp.int32, (TQ, LANES), 1)
    for g in range(GROUPS):
        out = jnp.where(lane < HEAD_DIM, states[2 * g][1], states[2 * g + 1][1])
        o_ref[0, :, g * LANES:(g + 1) * LANES] = out.astype(BF16)


def _sb_attention(qkv, b, s):
    n_blk = SB_HEADS * HEAD_DIM // (GROUPS * LANES)
    width = GROUPS * LANES
    return pl.pallas_call(
        _sb_kernel,
        grid=(b, n_blk, s // TQ),
        in_specs=[
            pl.BlockSpec((1, TQ, width), lambda bi, p, qi: (bi, qi, p)),
            pl.BlockSpec((1, s, width), lambda bi, p, qi: (bi, 0, n_blk + p)),
            pl.BlockSpec((1, s, width), lambda bi, p, qi: (bi, 0, 2 * n_blk + p)),
        ],
        out_specs=pl.BlockSpec((1, TQ, width), lambda bi, p, qi: (bi, qi, p)),
        out_shape=jax.ShapeDtypeStruct((b, s, SB_HEADS * HEAD_DIM), BF16),
        compiler_params=pltpu.CompilerParams(
            dimension_semantics=("parallel", "parallel", "arbitrary"),
            vmem_limit_bytes=VMEM_LIMIT),
        name="sb_attention",
    )(qkv, qkv, qkv)


def _softmax_tiles(qs, ks, vs, biases, states, mask):
    n = len(qs)
    ss = []
    for i in range(n):
        s = _dot_nt(qs[i], ks[i]) + biases[i]
        if mask is not None:
            s = jnp.where(mask, s, MASKED)
        ss.append(s)
    ps, alphas, new = [], [], []
    for i in range(n):
        m, l, _ = states[i]
        m_new = jnp.maximum(m, jnp.max(ss[i], axis=1, keepdims=True))
        alpha = jnp.exp(m - m_new)
        p = jnp.exp(ss[i] - m_new)
        new.append([m_new, alpha * l + jnp.sum(p, axis=1, keepdims=True)])
        alphas.append(alpha)
        ps.append(p.astype(BF16))
    for i in range(n):
        new[i].append(alphas[i] * states[i][2] + _dot(ps[i], vs[i]))
    return tuple(tuple(st) for st in new)


def _diff_kernel(q_ref, k_ref, v_ref, bias_ref, lq1_ref, lk1_ref, lq2_ref, lk2_ref, g_ref,
                 o_ref, *, lam_init):
    qi = pl.program_id(2)
    row = lax.broadcasted_iota(jnp.int32, (TQ, TK), 0)
    col = lax.broadcasted_iota(jnp.int32, (TQ, TK), 1)
    allowed = col < (row // CHUNK + 1) * CHUNK
    qs = _half_lane_queries(q_ref)

    def tiles(kb, states, which, mask):
        ks = [_key_tile(k_ref, kb, i // 2) for i in range(2 * GROUPS)]
        vs = [_key_tile(v_ref, kb, i // 2) for i in range(2 * GROUPS)]
        if which is None:
            biases = [bias_ref[i // 2, 1, TQ - 1:TQ, 0:1] for i in range(2 * GROUPS)]
        else:
            biases = [bias_ref[i // 2, which] for i in range(2 * GROUPS)]
        return _softmax_tiles(qs, ks, vs, biases, states, mask)

    init = tuple((jnp.full((TQ, 1), MASKED, F32), jnp.zeros((TQ, 1), F32),
                  jnp.zeros((TQ, LANES), F32)) for _ in range(2 * GROUPS))
    states = tiles(qi, init, 0, allowed)
    states = lax.cond(qi >= 1, lambda st: tiles(qi - 1, st, 1, None), lambda st: st, states)
    states = lax.fori_loop(0, jnp.maximum(qi - 1, 0),
                           lambda j, st: tiles(qi - 2 - j, st, None, None), states)

    lam = (jnp.exp(jnp.sum(lq1_ref[...] * lk1_ref[...], axis=1, keepdims=True))
           - jnp.exp(jnp.sum(lq2_ref[...] * lk2_ref[...], axis=1, keepdims=True))
           + lam_init)
    for g in range(GROUPS):
        (_, l1, acc1), (_, l2, acc2) = states[2 * g], states[2 * g + 1]
        y = acc1 * (1.0 / l1) - lam * (acc2 * (1.0 / l2))
        o_ref[0, :, g * LANES:(g + 1) * LANES] = (
            _rms(y, g_ref[...]) * (1.0 - lam_init)).astype(BF16)


def _diff_attention(qkv, bias_tiles, lq1, lk1, lq2, lk2, g_subln, b, s, lam_init):
    n_blk = DIFF_HEADS // GROUPS
    first = 3 * SB_HEADS * HEAD_DIM // (GROUPS * LANES)
    width = GROUPS * LANES
    vec = pl.BlockSpec((1, HEAD_DIM), lambda bi, h, qi: (0, 0))
    return pl.pallas_call(
        functools.partial(_diff_kernel, lam_init=lam_init),
        grid=(b, n_blk, s // TQ),
        in_specs=[
            pl.BlockSpec((1, TQ, width), lambda bi, h, qi: (bi, qi, first + h)),
            pl.BlockSpec((1, s, width), lambda bi, h, qi: (bi, 0, first + n_blk + h)),
            pl.BlockSpec((1, s, width), lambda bi, h, qi: (bi, 0, first + 2 * n_blk + h)),
            pl.BlockSpec((GROUPS, 2, TQ, TK), lambda bi, h, qi: (h, 0, 0, 0)),
            vec, vec, vec, vec,
            pl.BlockSpec((1, LANES), lambda bi, h, qi: (0, 0)),
        ],
        out_specs=pl.BlockSpec((1, TQ, width), lambda bi, h, qi: (bi, qi, h)),
        out_shape=jax.ShapeDtypeStruct((b, s, DIFF_HEADS * LANES), BF16),
        compiler_params=pltpu.CompilerParams(
            dimension_semantics=("parallel", "parallel", "arbitrary"),
            vmem_limit_bytes=VMEM_LIMIT),
        name="diff_attention",
    )(qkv, qkv, qkv, bias_tiles, lq1, lk1, lq2, lk2, g_subln)


def _t5_bucket(rel):
    half = REL_BUCKETS // 2
    max_exact = half // 2
    ret = jnp.where(rel > 0, half, 0)
    n = jnp.abs(rel)
    nf = jnp.maximum(n, 1).astype(F32)
    large = max_exact + (jnp.log(nf / max_exact) / math.log(REL_MAX_DIST / max_exact)
                         * (half - max_exact)).astype(jnp.int32)
    large = jnp.minimum(large, half - 1)
    return ret + jnp.where(n < max_exact, n, large)


def _toeplitz(vals, first):
    period = vals.shape[1]
    rolled = jnp.roll(vals, -first, axis=1)
    skew = jnp.tile(rolled, (1, TQ))[:, :TQ * (period - 1)].reshape(-1, TQ, period - 1)
    return skew[:, :, :TK]


def _bias_tiles(rel_bias):
    rel = jnp.arange(-2 * TQ + 1, TK + 1, dtype=jnp.int32)
    by_rel = rel_bias.astype(F32)[_t5_bucket(rel)].T
    diag = _toeplitz(by_rel, 2 * TQ - 1)
    older = _toeplitz(by_rel, 2 * TQ - 1 - TK)
    return jnp.stack([diag, older], axis=1)


def _merge_kernel(x_ref, ysb_ref, ydf_ref, gpre_ref, wg_ref, wsb_ref, wdf_ref, wo_ref,
                  gpost_ref, o_ref):
    d = x_ref.shape[1]
    x = x_ref[...]
    h = _rms(x, gpre_ref[...]).astype(BF16)
    gates = jax.nn.sigmoid(_dot(h, wg_ref[...]))
    merged = (gates[:, :d] * _dot(ysb_ref[...], wsb_ref[...])
              + gates[:, d:] * _dot(ydf_ref[...], wdf_ref[...]))
    o = _dot(merged.astype(BF16), wo_ref[...])
    o_ref[...] = x + _rms(o, gpost_ref[...])


def _merge(x2, ysb, ydf, gpre, wg, wsb, wdf, wo, gpost, tm):
    m, d = x2.shape

    def full(a):
        return pl.BlockSpec(a.shape, lambda i: (0, 0))

    def rows(a):
        return pl.BlockSpec((tm, a.shape[1]), lambda i: (i, 0))

    return pl.pallas_call(
        _merge_kernel,
        grid=(m // tm,),
        in_specs=[rows(x2), rows(ysb), rows(ydf), full(gpre), full(wg), full(wsb), full(wdf),
                  full(wo), full(gpost)],
        out_specs=pl.BlockSpec((tm, d), lambda i: (i, 0)),
        out_shape=jax.ShapeDtypeStruct((m, d), F32),
        compiler_params=pltpu.CompilerParams(
            dimension_semantics=("parallel",),
            vmem_limit_bytes=VMEM_LIMIT),
        name="merge",
    )(x2, ysb, ydf, gpre, wg, wsb, wdf, wo, gpost)


def _mlp_kernel(x_ref, gpre_ref, wup_ref, wdn_ref, gpost_ref, o_ref, h_ref, acc_ref):
    f = pl.program_id(1)

    @pl.when(f == 0)
    def _():
        h_ref[...] = _rms(x_ref[...], gpre_ref[...]).astype(BF16)
        acc_ref[...] = jnp.zeros_like(acc_ref)

    u = jnp.square(jnp.maximum(_dot(h_ref[...], wup_ref[...]), 0.0))
    acc_ref[...] += _dot(u.astype(BF16), wdn_ref[...])

    @pl.when(f == pl.num_programs(1) - 1)
    def _():
        o_ref[...] = x_ref[...] + _rms(acc_ref[...], gpost_ref[...])


def _mlp(x2, gpre, wup, wdn, gpost, tm, tf):
    m, d = x2.shape
    ff = wup.shape[1]
    return pl.pallas_call(
        _mlp_kernel,
        grid=(m // tm, ff // tf),
        in_specs=[
            pl.BlockSpec((tm, d), lambda i, f: (i, 0)),
            pl.BlockSpec((1, d), lambda i, f: (0, 0)),
            pl.BlockSpec((d, tf), lambda i, f: (0, f)),
            pl.BlockSpec((tf, d), lambda i, f: (f, 0)),
            pl.BlockSpec((1, d), lambda i, f: (0, 0)),
        ],
        out_specs=pl.BlockSpec((tm, d), lambda i, f: (i, 0)),
        out_shape=jax.ShapeDtypeStruct((m, d), F32),
        scratch_shapes=[pltpu.VMEM((tm, d), BF16), pltpu.VMEM((tm, d), F32)],
        compiler_params=pltpu.CompilerParams(
            dimension_semantics=("parallel", "arbitrary"),
            vmem_limit_bytes=VMEM_LIMIT),
        name="mlp",
    )(x2, gpre, wup, wdn, gpost)


def kernel(x, w_in, w_sb_out, w_diff_out, w_o, lambda_q1, lambda_k1, lambda_q2, lambda_k2, w_subln, rel_bias, g_pre_mix, g_post_mix, g_pre_mlp, g_post_mlp, w_up, w_down):
    b, s, d = x.shape
    depth = w_in.shape[0]
    sb_width = SB_HEADS * HEAD_DIM
    diff_width = DIFF_HEADS * LANES
    n_qkv = 3 * sb_width + 3 * diff_width
    scale = HEAD_DIM ** -0.5
    col_scale = jnp.ones((n_qkv,), F32)
    col_scale = col_scale.at[:sb_width].set(scale)
    col_scale = col_scale.at[3 * sb_width:3 * sb_width + diff_width].set(scale)
    bias_tiles = _bias_tiles(rel_bias)

    x2 = x.reshape(b * s, d)
    for l in range(depth):
        lam_init = 0.8 - 0.6 * math.exp(-0.3 * l)
        w_qkv = (w_in[l][:, :n_qkv] * col_scale).astype(BF16)
        w_gate = w_in[l][:, n_qkv:].astype(BF16)
        qkv = _proj(x2, g_pre_mix[l][None], w_qkv, 512, 1024).reshape(b, s, n_qkv)
        y_sb = _sb_attention(qkv, b, s)
        y_diff = _diff_attention(qkv, bias_tiles, lambda_q1[l][None], lambda_k1[l][None],
                                 lambda_q2[l][None], lambda_k2[l][None], w_subln[l][None],
                                 b, s, lam_init)
        x2 = _merge(x2, y_sb.reshape(b * s, sb_width), y_diff.reshape(b * s, diff_width),
                    g_pre_mix[l][None], w_gate, w_sb_out[l].astype(BF16),
                    w_diff_out[l].astype(BF16), w_o[l].astype(BF16), g_post_mix[l][None], 256)
        x2 = _mlp(x2, g_pre_mlp[l][None], w_up[l].astype(BF16), w_down[l].astype(BF16),
                  g_post_mlp[l][None], 512, 1024)
    return x2.reshape(b, s, d)
```

```python
import functools
import math

import jax
import jax.numpy as jnp
from jax import lax
from jax.experimental import pallas as pl
from jax.experimental.pallas import tpu as pltpu

F32 = jnp.float32
BF16 = jnp.bfloat16

NORM_EPS = 1e-6
CHUNK = 64
SB_HEADS = 8
HEAD_DIM = 64
DIFF_HEADS = 4
LANES = 128
REL_BUCKETS = 32
REL_MAX_DIST = 128
MASKED = -1e30
LOG2E = 1.4426950408889634
DEAD_LOG2 = -160.0

VMEM_LIMIT = 56 * 1024 * 1024

TQ = 256
TK = 256
GROUPS = 2


def _rms(xf, g):
    ms = jnp.mean(xf * xf, axis=-1, keepdims=True)
    return xf * lax.rsqrt(ms + NORM_EPS) * g


def _dot(a, b):
    return jnp.dot(a, b, preferred_element_type=F32)


def _dot_nt(a, b):
    return lax.dot_general(a, b, (((1,), (1,)), ((), ())), preferred_element_type=F32)


def _neg_abs(x):
    bits = lax.bitcast_convert_type(x, jnp.uint32) | jnp.uint32(0x80000000)
    return lax.bitcast_convert_type(bits, F32)


def _half_lane_queries(q_ref):
    lane = lax.broadcasted_iota(jnp.int32, (TQ, LANES), 1)
    qs = []
    for g in range(GROUPS):
        q_all = q_ref[0, :, g * LANES:(g + 1) * LANES]
        for half in range(2):
            keep = (lane >= half * HEAD_DIM) & (lane < (half + 1) * HEAD_DIM)
            qs.append(jnp.where(keep, q_all, jnp.zeros_like(q_all)))
    return qs


def _key_tile(ref, kb, g):
    start = pl.multiple_of(kb * TK, TK)
    return ref[0, pl.ds(start, TK), g * LANES:(g + 1) * LANES]


def _proj_kernel(x_ref, g_ref, w_ref, o_ref, h_ref):
    @pl.when(pl.program_id(1) == 0)
    def _():
        h_ref[...] = _rms(x_ref[...], g_ref[...]).astype(BF16)

    o_ref[...] = _dot(h_ref[...], w_ref[...]).astype(BF16)


def _proj(x2, g, w, tm, tn):
    m, d = x2.shape
    n = w.shape[1]
    return pl.pallas_call(
        _proj_kernel,
        grid=(m // tm, n // tn),
        in_specs=[
            pl.BlockSpec((tm, d), lambda i, j: (i, 0)),
            pl.BlockSpec((1, d), lambda i, j: (0, 0)),
            pl.BlockSpec((d, tn), lambda i, j: (0, j)),
        ],
        out_specs=pl.BlockSpec((tm, tn), lambda i, j: (i, j)),
        out_shape=jax.ShapeDtypeStruct((m, n), BF16),
        scratch_shapes=[pltpu.VMEM((tm, d), BF16)],
        compiler_params=pltpu.CompilerParams(
            dimension_semantics=("parallel", "arbitrary"),
            vmem_limit_bytes=VMEM_LIMIT),
        name="proj",
    )(x2, g, w)


def _sb_tiles(qs, ks, vs, upper, states, mask):
    n = len(qs)
    zs = [_dot_nt(qs[i], ks[i]) * LOG2E for i in range(n)]
    log_betas, log_oms, his, los = [], [], [], []
    for z in zs:
        soft = jnp.log2(1.0 + jnp.exp2(_neg_abs(z)))
        log_beta = jnp.minimum(z, 0.0) - soft
        log_om = log_beta - z
        if mask is not None:
            log_om = jnp.where(mask, log_om, 0.0)
        hi = log_om.astype(BF16)
        log_betas.append(log_beta)
        log_oms.append(log_om)
        his.append(hi)
        los.append((log_om - hi.astype(F32)).astype(BF16))
    tails = [_dot(his[i], upper) + _dot(los[i], upper) for i in range(n)]
    ws = []
    for i in range(n):
        w = jnp.exp2(log_betas[i] + tails[i])
        if mask is not None:
            w = jnp.where(mask, w, 0.0)
        ws.append(w.astype(BF16))
    new = []
    for i in range(n):
        c, acc = states[i]
        acc = acc + jnp.exp2(c) * _dot(ws[i], vs[i])
        c = c + jnp.sum(log_oms[i], axis=1, keepdims=True)
        new.append((c, acc))
    return tuple(new)


def _sb_kernel(q_ref, k_ref, v_ref, o_ref):
    qi = pl.program_id(2)
    row = lax.broadcasted_iota(jnp.int32, (TQ, TK), 0)
    col = lax.broadcasted_iota(jnp.int32, (TQ, TK), 1)
    causal = col < row
    upper = jnp.where(row > col, 1.0, 0.0).astype(BF16)
    qs = _half_lane_queries(q_ref)

    def tiles(kb, states, mask):
        ks = [_key_tile(k_ref, kb, i // 2) for i in range(2 * GROUPS)]
        vs = [_key_tile(v_ref, kb, i // 2) for i in range(2 * GROUPS)]
        return _sb_tiles(qs, ks, vs, upper, states, mask)

    init = tuple((jnp.zeros((TQ, 1), F32), jnp.zeros((TQ, LANES), F32))
                 for _ in range(2 * GROUPS))
    def any_weight_left(states):
        c_max = functools.reduce(jnp.maximum, [jnp.max(st[0]) for st in states])
        return c_max > DEAD_LOG2

    def step(carry):
        j, _, states = carry
        states = tiles(qi - 1 - j, states, None)
        return j + 1, any_weight_left(states), states

    states = tiles(qi, init, causal)
    _, _, states = lax.while_loop(lambda carry: (carry[0] < qi) & carry[1], step,
                                  (jnp.int32(0), any_weight_left(states), states))

    lane = lax.broadcasted_iota(jnp.int32, (TQ, LANES), 1)
    for g in range(GROUPS):
        out = jnp.where(lane < HEAD_DIM, states[2 * g][1], states[2 * g + 1][1])
        o_ref[0, :, g * LANES:(g + 1) * LANES] = out.astype(BF16)


def _sb_attention(qkv, b, s):
    n_blk = SB_HEADS * HEAD_DIM // (GROUPS * LANES)
    width = GROUPS * LANES
    return pl.pallas_call(
        _sb_kernel,
        grid=(b, n_blk, s // TQ),
        in_specs=[
            pl.BlockSpec((1, TQ, width), lambda bi, p, qi: (bi, qi, p)),
            pl.BlockSpec((1, s, width), lambda bi, p, qi: (bi, 0, n_blk + p)),
            pl.BlockSpec((1, s, width), lambda bi, p, qi: (bi, 0, 2 * n_blk + p)),
        ],
        out_specs=pl.BlockSpec((1, TQ, width), lambda bi, p, qi: (bi, qi, p)),
        out_shape=jax.ShapeDtypeStruct((b, s, SB_HEADS * HEAD_DIM), BF16),
        compiler_params=pltpu.CompilerParams(
            dimension_semantics=("parallel", "parallel", "arbitrary"),
            vmem_limit_bytes=VMEM_LIMIT),
        name="sb_attention",
    )(qkv, qkv, qkv)


BIAS_DIAG, BIAS_NEAR, BIAS_FAR, BIAS_DEAD = 0, 1, 2, 3
ONES_ROWS = 16
VT_ROWS = LANES + ONES_ROWS


def _diff_kernel(qt_ref, k_ref, vt_ref, bias_ref, lq1_ref, lk1_ref, lq2_ref, lk2_ref, g_ref,
                 o_ref, z_a, z_b, p_a, p_b, m_ref, alpha_a, alpha_b, acc_ref, *, lam_init):
    qi = pl.program_id(2)
    n_chains = 2 * GROUPS
    feature = lax.broadcasted_iota(jnp.int32, (LANES, TQ), 0)
    qts = []
    for g in range(GROUPS):
        qt_all = qt_ref[0, 0, g * LANES:(g + 1) * LANES, :]
        for half in range(2):
            keep = (feature >= half * HEAD_DIM) & (feature < (half + 1) * HEAD_DIM)
            qts.append(jnp.where(keep, qt_all, jnp.zeros_like(qt_all)))

    def key_block(j):
        return jnp.clip(qi - j, 0, qi)

    def bias_kind(j):
        return jnp.where(j > qi, BIAS_DEAD, jnp.minimum(j, BIAS_FAR))

    def scores(j, z_ref):
        kb = key_block(j)
        for i in range(n_chains):
            z_ref[i] = _dot(_key_tile(k_ref, kb, i // 2), qts[i])

    def normalise(j, z_ref, p_ref, alpha_ref, per_element_bias):
        kind = bias_kind(j)
        for i in range(n_chains):
            m_old = m_ref[i]
            if per_element_bias:
                bias = bias_ref.at[i // 2, kind]
                m_new = jnp.maximum(m_old, jnp.max(z_ref[i] + bias[...], axis=0, keepdims=True))
                m_ref[i] = m_new
                p = jnp.exp(z_ref[i] + bias[...] - m_new)
            else:
                c = bias_ref[i // 2, kind, 0:1, 0:1]
                m_new = jnp.maximum(m_old, jnp.max(z_ref[i], axis=0, keepdims=True) + c)
                m_ref[i] = m_new
                p = jnp.exp(z_ref[i] - (m_new - c))
            alpha_ref[i] = jnp.exp(m_old - m_new)
            p_ref[i] = p.astype(BF16)

    def accumulate(j, p_ref, alpha_ref):
        kb = key_block(j)
        for i in range(n_chains):
            g = i // 2
            vt = vt_ref[0, kb, g * VT_ROWS:(g + 1) * VT_ROWS, :]
            acc_ref[i] = alpha_ref[i] * acc_ref[i] + _dot(vt, p_ref[i])

    m_ref[...] = jnp.full(m_ref.shape, MASKED, F32)
    acc_ref[...] = jnp.zeros(acc_ref.shape, F32)
    scores(0, z_a)
    normalise(0, z_a, p_a, alpha_a, True)
    scores(1, z_b)
    accumulate(0, p_a, alpha_a)
    normalise(1, z_b, p_b, alpha_b, True)
    scores(2, z_a)

    def trip(t, carry):
        j = 2 * t
        accumulate(j - 1, p_b, alpha_b)
        normalise(j, z_a, p_a, alpha_a, False)
        scores(j + 1, z_b)
        accumulate(j, p_a, alpha_a)
        normalise(j + 1, z_b, p_b, alpha_b, False)
        scores(j + 2, z_a)
        return carry

    n_trips = (qi + 2) // 2
    lax.fori_loop(1, n_trips, trip, 0)
    accumulate(2 * n_trips - 1, p_b, alpha_b)

    lam = (jnp.exp(jnp.sum(lq1_ref[...] * lk1_ref[...], axis=1, keepdims=True))
           - jnp.exp(jnp.sum(lq2_ref[...] * lk2_ref[...], axis=1, keepdims=True))
           + lam_init)
    for g in range(GROUPS):
        normed = []
        for i in (2 * g, 2 * g + 1):
            normed.append(acc_ref[i, :LANES, :] * (1.0 / acc_ref[i, LANES:LANES + 1, :]))
        yt = normed[0] - lam * normed[1]
        ms = jnp.mean(yt * yt, axis=0, keepdims=True)
        yt = yt * lax.rsqrt(ms + NORM_EPS) * g_ref[...] * (1.0 - lam_init)
        o_ref[0, :, g * LANES:(g + 1) * LANES] = yt.T.astype(BF16)


def _diff_attention(qt, qkv, vt, bias_tiles, lq1, lk1, lq2, lk2, g_subln, b, s, lam_init):
    n_blk = DIFF_HEADS // GROUPS
    first = 3 * SB_HEADS * HEAD_DIM // (GROUPS * LANES)
    width = GROUPS * LANES
    n_chains = 2 * GROUPS
    vec = pl.BlockSpec((1, HEAD_DIM), lambda bi, h, qi: (0, 0))
    return pl.pallas_call(
        functools.partial(_diff_kernel, lam_init=lam_init),
        grid=(b, n_blk, s // TQ),
        in_specs=[
            pl.BlockSpec((1, 1, width, TQ), lambda bi, h, qi: (bi, qi, h, 0)),
            pl.BlockSpec((1, s, width), lambda bi, h, qi: (bi, 0, first + n_blk + h)),
            pl.BlockSpec((1, s // TK, GROUPS * VT_ROWS, TK), lambda bi, h, qi: (bi, 0, h, 0)),
            pl.BlockSpec((GROUPS, 4, TK, TQ), lambda bi, h, qi: (h, 0, 0, 0)),
            vec, vec, vec, vec,
            pl.BlockSpec((LANES, 1), lambda bi, h, qi: (0, 0)),
        ],
        out_specs=pl.BlockSpec((1, TQ, width), lambda bi, h, qi: (bi, qi, h)),
        out_shape=jax.ShapeDtypeStruct((b, s, DIFF_HEADS * LANES), BF16),
        scratch_shapes=[
            pltpu.VMEM((n_chains, TK, TQ), F32), pltpu.VMEM((n_chains, TK, TQ), F32),
            pltpu.VMEM((n_chains, TK, TQ), BF16), pltpu.VMEM((n_chains, TK, TQ), BF16),
            pltpu.VMEM((n_chains, 1, TQ), F32),
            pltpu.VMEM((n_chains, 1, TQ), F32), pltpu.VMEM((n_chains, 1, TQ), F32),
            pltpu.VMEM((n_chains, VT_ROWS, TQ), F32),
        ],
        compiler_params=pltpu.CompilerParams(
            dimension_semantics=("parallel", "parallel", "arbitrary"),
            vmem_limit_bytes=VMEM_LIMIT),
        name="diff_attention",
    )(qt, qkv, vt, bias_tiles, lq1, lk1, lq2, lk2, g_subln)


def _t5_bucket(rel):
    half = REL_BUCKETS // 2
    max_exact = half // 2
    ret = jnp.where(rel > 0, half, 0)
    n = jnp.abs(rel)
    nf = jnp.maximum(n, 1).astype(F32)
    large = max_exact + (jnp.log(nf / max_exact) / math.log(REL_MAX_DIST / max_exact)
                         * (half - max_exact)).astype(jnp.int32)
    large = jnp.minimum(large, half - 1)
    return ret + jnp.where(n < max_exact, n, large)


def _toeplitz(vals, first):
    period = vals.shape[1]
    rolled = jnp.roll(vals, -first, axis=1)
    skew = jnp.tile(rolled, (1, TQ))[:, :TQ * (period - 1)].reshape(-1, TQ, period - 1)
    return skew[:, :, :TK]


def _bias_tiles(rel_bias):
    rel = jnp.arange(-2 * TQ + 1, TK + 1, dtype=jnp.int32)
    by_rel = rel_bias.astype(F32)[_t5_bucket(rel)].T
    key = jnp.arange(TK)[:, None]
    query = jnp.arange(TQ)[None, :]
    allowed = key < (query // CHUNK + 1) * CHUNK
    diag = jnp.where(allowed, jnp.swapaxes(_toeplitz(by_rel, 2 * TQ - 1), 1, 2), MASKED)
    near = jnp.swapaxes(_toeplitz(by_rel, 2 * TQ - 1 - TK), 1, 2)
    far = jnp.broadcast_to(by_rel[:, :1, None], near.shape)
    dead = jnp.full(near.shape, MASKED, F32)
    return jnp.stack([diag, near, far, dead], axis=1)


def _feature_major(x, b, s, tile):
    return jnp.swapaxes(x.reshape(b, s // tile, tile, x.shape[-1]), 2, 3)


def _merge_kernel(x_ref, ysb_ref, ydf_ref, gpre_ref, wg_ref, wsb_ref, wdf_ref, wo_ref,
                  gpost_ref, o_ref):
    d = x_ref.shape[1]
    x = x_ref[...]
    h = _rms(x, gpre_ref[...]).astype(BF16)
    gates = jax.nn.sigmoid(_dot(h, wg_ref[...]))
    merged = (gates[:, :d] * _dot(ysb_ref[...], wsb_ref[...])
              + gates[:, d:] * _dot(ydf_ref[...], wdf_ref[...]))
    o = _dot(merged.astype(BF16), wo_ref[...])
    o_ref[...] = x + _rms(o, gpost_ref[...])


def _merge(x2, ysb, ydf, gpre, wg, wsb, wdf, wo, gpost, tm):
    m, d = x2.shape

    def full(a):
        return pl.BlockSpec(a.shape, lambda i: (0, 0))

    def rows(a):
        return pl.BlockSpec((tm, a.shape[1]), lambda i: (i, 0))

    return pl.pallas_call(
        _merge_kernel,
        grid=(m // tm,),
        in_specs=[rows(x2), rows(ysb), rows(ydf), full(gpre), full(wg), full(wsb), full(wdf),
                  full(wo), full(gpost)],
        out_specs=pl.BlockSpec((tm, d), lambda i: (i, 0)),
        out_shape=jax.ShapeDtypeStruct((m, d), F32),
        compiler_params=pltpu.CompilerParams(
            dimension_semantics=("parallel",),
            vmem_limit_bytes=VMEM_LIMIT),
        name="merge",
    )(x2, ysb, ydf, gpre, wg, wsb, wdf, wo, gpost)


def _mlp_kernel(x_ref, gpre_ref, wup_ref, wdn_ref, gpost_ref, o_ref, h_ref, acc_ref):
    f = pl.program_id(1)

    @pl.when(f == 0)
    def _():
        h_ref[...] = _rms(x_ref[...], gpre_ref[...]).astype(BF16)
        acc_ref[...] = jnp.zeros_like(acc_ref)

    u = jnp.square(jnp.maximum(_dot(h_ref[...], wup_ref[...]), 0.0))
    acc_ref[...] += _dot(u.astype(BF16), wdn_ref[...])

    @pl.when(f == pl.num_programs(1) - 1)
    def _():
        o_ref[...] = x_ref[...] + _rms(acc_ref[...], gpost_ref[...])


def _mlp(x2, gpre, wup, wdn, gpost, tm, tf):
    m, d = x2.shape
    ff = wup.shape[1]
    return pl.pallas_call(
        _mlp_kernel,
        grid=(m // tm, ff // tf),
        in_specs=[
            pl.BlockSpec((tm, d), lambda i, f: (i, 0)),
            pl.BlockSpec((1, d), lambda i, f: (0, 0)),
            pl.BlockSpec((d, tf), lambda i, f: (0, f)),
            pl.BlockSpec((tf, d), lambda i, f: (f, 0)),
            pl.BlockSpec((1, d), lambda i, f: (0, 0)),
        ],
        out_specs=pl.BlockSpec((tm, d), lambda i, f: (i, 0)),
        out_shape=jax.ShapeDtypeStruct((m, d), F32),
        scratch_shapes=[pltpu.VMEM((tm, d), BF16), pltpu.VMEM((tm, d), F32)],
        compiler_params=pltpu.CompilerParams(
            dimension_semantics=("parallel", "arbitrary"),
            vmem_limit_bytes=VMEM_LIMIT),
        name="mlp",
    )(x2, gpre, wup, wdn, gpost)


def kernel(x, w_in, w_sb_out, w_diff_out, w_o, lambda_q1, lambda_k1, lambda_q2, lambda_k2, w_subln, rel_bias, g_pre_mix, g_post_mix, g_pre_mlp, g_post_mlp, w_up, w_down):
    b, s, d = x.shape
    depth = w_in.shape[0]
    sb_width = SB_HEADS * HEAD_DIM
    diff_width = DIFF_HEADS * LANES
    n_qkv = 3 * sb_width + 3 * diff_width
    scale = HEAD_DIM ** -0.5
    col_scale = jnp.ones((n_qkv,), F32)
    col_scale = col_scale.at[:sb_width].set(scale)
    col_scale = col_scale.at[3 * sb_width:3 * sb_width + diff_width].set(scale)
    bias_tiles = _bias_tiles(rel_bias)

    x2 = x.reshape(b * s, d)
    for l in range(depth):
        lam_init = 0.8 - 0.6 * math.exp(-0.3 * l)
        w_qkv = (w_in[l][:, :n_qkv] * col_scale).astype(BF16)
        w_gate = w_in[l][:, n_qkv:].astype(BF16)
        qkv = _proj(x2, g_pre_mix[l][None], w_qkv, 512, 1024).reshape(b, s, n_qkv)
        y_sb = _sb_attention(qkv, b, s)
        d_q = qkv[..., 3 * sb_width:3 * sb_width + diff_width]
        d_v = qkv[..., 3 * sb_width + 2 * diff_width:]
        v_t = _feature_major(d_v, b, s, TK).reshape(b, s // TK, DIFF_HEADS, LANES, TK)
        v_t = jnp.concatenate(
            [v_t, jnp.ones((b, s // TK, DIFF_HEADS, ONES_ROWS, TK), BF16)], axis=3)
        y_diff = _diff_attention(_feature_major(d_q, b, s, TQ), qkv,
                                 v_t.reshape(b, s // TK, DIFF_HEADS * VT_ROWS, TK),
                                 bias_tiles, lambda_q1[l][None], lambda_k1[l][None],
                                 lambda_q2[l][None], lambda_k2[l][None], w_subln[l][:, None],
                                 b, s, lam_init)
        x2 = _merge(x2, y_sb.reshape(b * s, sb_width), y_diff.reshape(b * s, diff_width),
                    g_pre_mix[l][None], w_gate, w_sb_out[l].astype(BF16),
                    w_diff_out[l].astype(BF16), w_o[l].astype(BF16), g_post_mix[l][None], 256)
        x2 = _mlp(x2, g_pre_mlp[l][None], w_up[l].astype(BF16), w_down[l].astype(BF16),
                  g_post_mlp[l][None], 512, 1024)
    return x2.reshape(b, s, d)
```

```python
import functools
import math

import jax
import jax.numpy as jnp
from jax import lax
from jax.experimental import pallas as pl
from jax.experimental.pallas import tpu as pltpu

F32 = jnp.float32
BF16 = jnp.bfloat16

NORM_EPS = 1e-6
CHUNK = 64
SB_HEADS = 8
HEAD_DIM = 64
DIFF_HEADS = 4
LANES = 128
REL_BUCKETS = 32
REL_MAX_DIST = 128
MASKED = -1e30
LOG2E = 1.4426950408889634
DEAD_LOG2 = -160.0

VMEM_LIMIT = 56 * 1024 * 1024

TQ = 256
TK = 256
GROUPS = 2


def _rms(xf, g):
    ms = jnp.mean(xf * xf, axis=-1, keepdims=True)
    return xf * lax.rsqrt(ms + NORM_EPS) * g


def _dot(a, b):
    return jnp.dot(a, b, preferred_element_type=F32)


def _dot_nt(a, b):
    return lax.dot_general(a, b, (((1,), (1,)), ((), ())), preferred_element_type=F32)


def _neg_abs(x):
    bits = lax.bitcast_convert_type(x, jnp.uint32) | jnp.uint32(0x80000000)
    return lax.bitcast_convert_type(bits, F32)


def _half_lane_queries(q_ref):
    lane = lax.broadcasted_iota(jnp.int32, (TQ, LANES), 1)
    qs = []
    for g in range(GROUPS):
        q_all = q_ref[0, :, g * LANES:(g + 1) * LANES]
        for half in range(2):
            keep = (lane >= half * HEAD_DIM) & (lane < (half + 1) * HEAD_DIM)
            qs.append(jnp.where(keep, q_all, jnp.zeros_like(q_all)))
    return qs


def _key_tile(ref, kb, g):
    start = pl.multiple_of(kb * TK, TK)
    return ref[0, pl.ds(start, TK), g * LANES:(g + 1) * LANES]


ROW_BLOCK = 256
COL_BLOCK = 1024


def _row_blocks(tm):
    return [slice(r, r + ROW_BLOCK) for r in range(0, tm, ROW_BLOCK)]


def _proj_kernel(x_ref, g_ref, w_ref, o_ref):
    blocks = _row_blocks(x_ref.shape[0])
    g = g_ref[...]
    h = _rms(x_ref[blocks[0], :], g).astype(BF16)
    for k, rows in enumerate(blocks):
        h_now = h
        if k + 1 < len(blocks):
            h = _rms(x_ref[blocks[k + 1], :], g).astype(BF16)
        for c in range(0, w_ref.shape[1], COL_BLOCK):
            o_ref[rows, c:c + COL_BLOCK] = _dot(h_now, w_ref[:, c:c + COL_BLOCK]).astype(BF16)


def _proj(x2, g, w, tm):
    m, d = x2.shape
    n = w.shape[1]
    return pl.pallas_call(
        _proj_kernel,
        grid=(m // tm,),
        in_specs=[
            pl.BlockSpec((tm, d), lambda i: (i, 0)),
            pl.BlockSpec((1, d), lambda i: (0, 0)),
            pl.BlockSpec((d, n), lambda i: (0, 0)),
        ],
        out_specs=pl.BlockSpec((tm, n), lambda i: (i, 0)),
        out_shape=jax.ShapeDtypeStruct((m, n), BF16),
        compiler_params=pltpu.CompilerParams(
            dimension_semantics=("parallel",),
            vmem_limit_bytes=VMEM_LIMIT),
        name="proj",
    )(x2, g, w)


def _sb_tiles(qs, ks, vs, upper, states, mask):
    n = len(qs)
    zs = [_dot_nt(qs[i], ks[i]) * LOG2E for i in range(n)]
    log_betas, log_oms, his, los = [], [], [], []
    for z in zs:
        soft = jnp.log2(1.0 + jnp.exp2(_neg_abs(z)))
        log_beta = jnp.minimum(z, 0.0) - soft
        log_om = log_beta - z
        if mask is not None:
            log_om = jnp.where(mask, log_om, 0.0)
        hi = log_om.astype(BF16)
        log_betas.append(log_beta)
        log_oms.append(log_om)
        his.append(hi)
        los.append((log_om - hi.astype(F32)).astype(BF16))
    tails = [_dot(his[i], upper) + _dot(los[i], upper) for i in range(n)]
    ws = []
    for i in range(n):
        w = jnp.exp2(log_betas[i] + tails[i])
        if mask is not None:
            w = jnp.where(mask, w, 0.0)
        ws.append(w.astype(BF16))
    new = []
    for i in range(n):
        c, acc = states[i]
        acc = acc + jnp.exp2(c) * _dot(ws[i], vs[i])
        c = c + jnp.sum(log_oms[i], axis=1, keepdims=True)
        new.append((c, acc))
    return tuple(new)


def _sb_kernel(q_ref, k_ref, v_ref, o_ref):
    qi = pl.program_id(2)
    row = lax.broadcasted_iota(jnp.int32, (TQ, TK), 0)
    col = lax.broadcasted_iota(jnp.int32, (TQ, TK), 1)
    causal = col < row
    upper = jnp.where(row > col, 1.0, 0.0).astype(BF16)
    qs = _half_lane_queries(q_ref)

    def tiles(kb, states, mask):
        ks = [_key_tile(k_ref, kb, i // 2) for i in range(2 * GROUPS)]
        vs = [_key_tile(v_ref, kb, i // 2) for i in range(2 * GROUPS)]
        return _sb_tiles(qs, ks, vs, upper, states, mask)

    init = tuple((jnp.zeros((TQ, 1), F32), jnp.zeros((TQ, LANES), F32))
                 for _ in range(2 * GROUPS))
    def any_weight_left(states):
        c_max = functools.reduce(jnp.maximum, [jnp.max(st[0]) for st in states])
        return c_max > DEAD_LOG2

    def step(carry):
        j, _, states = carry
        states = tiles(qi - 1 - j, states, None)
        return j + 1, any_weight_left(states), states

    states = tiles(qi, init, causal)
    _, _, states = lax.while_loop(lambda carry: (carry[0] < qi) & carry[1], step,
                                  (jnp.int32(0), any_weight_left(states), states))

    lane = lax.broadcasted_iota(jnp.int32, (TQ, LANES), 1)
    for g in range(GROUPS):
        out = jnp.where(lane < HEAD_DIM, states[2 * g][1], states[2 * g + 1][1])
        o_ref[0, :, g * LANES:(g + 1) * LANES] = out.astype(BF16)


def _sb_attention(qkv, b, s):
    n_blk = SB_HEADS * HEAD_DIM // (GROUPS * LANES)
    width = GROUPS * LANES
    return pl.pallas_call(
        _sb_kernel,
        grid=(b, n_blk, s // TQ),
        in_specs=[
            pl.BlockSpec((1, TQ, width), lambda bi, p, qi: (bi, qi, p)),
            pl.BlockSpec((1, s, width), lambda bi, p, qi: (bi, 0, n_blk + p)),
            pl.BlockSpec((1, s, width), lambda bi, p, qi: (bi, 0, 2 * n_blk + p)),
        ],
        out_specs=pl.BlockSpec((1, TQ, width), lambda bi, p, qi: (bi, qi, p)),
        out_shape=jax.ShapeDtypeStruct((b, s, SB_HEADS * HEAD_DIM), BF16),
        compiler_params=pltpu.CompilerParams(
            dimension_semantics=("parallel", "parallel", "arbitrary"),
            vmem_limit_bytes=VMEM_LIMIT),
        name="sb_attention",
    )(qkv, qkv, qkv)


BIAS_DIAG, BIAS_NEAR, BIAS_FAR, BIAS_DEAD = 0, 1, 2, 3
ONES_ROWS = 16
VT_ROWS = LANES + ONES_ROWS


def _diff_kernel(qt_ref, k_ref, vt_ref, bias_ref, lq1_ref, lk1_ref, lq2_ref, lk2_ref, g_ref,
                 o_ref, z_a, z_b, p_a, p_b, m_ref, alpha_a, alpha_b, acc_ref, *, lam_init):
    qi = pl.program_id(2)
    n_chains = 2 * GROUPS
    feature = lax.broadcasted_iota(jnp.int32, (LANES, TQ), 0)
    qts = []
    for g in range(GROUPS):
        qt_all = qt_ref[0, 0, g * LANES:(g + 1) * LANES, :]
        for half in range(2):
            keep = (feature >= half * HEAD_DIM) & (feature < (half + 1) * HEAD_DIM)
            qts.append(jnp.where(keep, qt_all, jnp.zeros_like(qt_all)))

    def key_block(j):
        return jnp.clip(qi - j, 0, qi)

    def bias_kind(j):
        return jnp.where(j > qi, BIAS_DEAD, jnp.minimum(j, BIAS_FAR))

    def scores(j, z_ref, chains=range(2 * GROUPS)):
        kb = key_block(j)
        for i in chains:
            z_ref[i] = _dot(_key_tile(k_ref, kb, i // 2), qts[i])

    def normalise(j, z_ref, p_ref, alpha_ref, per_element_bias, chains=range(2 * GROUPS)):
        kind = bias_kind(j)
        for i in chains:
            m_old = m_ref[i]
            if per_element_bias:
                bias = bias_ref.at[i // 2, kind]
                m_new = jnp.maximum(m_old, jnp.max(z_ref[i] + bias[...], axis=0, keepdims=True))
                m_ref[i] = m_new
                p = jnp.exp(z_ref[i] + bias[...] - m_new)
            else:
                c = bias_ref[i // 2, kind, 0:1, 0:1]
                m_new = jnp.maximum(m_old, jnp.max(z_ref[i], axis=0, keepdims=True) + c)
                m_ref[i] = m_new
                p = jnp.exp(z_ref[i] - (m_new - c))
            alpha_ref[i] = jnp.exp(m_old - m_new)
            p_ref[i] = p.astype(BF16)

    def accumulate(j, p_ref, alpha_ref, chains=range(2 * GROUPS)):
        kb = key_block(j)
        for i in chains:
            g = i // 2
            vt = vt_ref[0, kb, g * VT_ROWS:(g + 1) * VT_ROWS, :]
            acc_ref[i] = alpha_ref[i] * acc_ref[i] + _dot(vt, p_ref[i])

    m_ref[...] = jnp.full(m_ref.shape, MASKED, F32)
    acc_ref[...] = jnp.zeros(acc_ref.shape, F32)
    scores(0, z_a)
    normalise(0, z_a, p_a, alpha_a, True)
    scores(1, z_b)
    accumulate(0, p_a, alpha_a)
    normalise(1, z_b, p_b, alpha_b, True)
    scores(2, z_a)

    def trip(t, carry):
        j = 2 * t
        for i in range(n_chains):
            normalise(j, z_a, p_a, alpha_a, False, (i,))
            accumulate(j - 1, p_b, alpha_b, (i,))
            scores(j + 1, z_b, (i,))
        for i in range(n_chains):
            normalise(j + 1, z_b, p_b, alpha_b, False, (i,))
            accumulate(j, p_a, alpha_a, (i,))
            scores(j + 2, z_a, (i,))
        return carry

    n_trips = (qi + 2) // 2
    lax.fori_loop(1, n_trips, trip, 0)
    accumulate(2 * n_trips - 1, p_b, alpha_b)

    lam = (jnp.exp(jnp.sum(lq1_ref[...] * lk1_ref[...], axis=1, keepdims=True))
           - jnp.exp(jnp.sum(lq2_ref[...] * lk2_ref[...], axis=1, keepdims=True))
           + lam_init)
    for g in range(GROUPS):
        normed = []
        for i in (2 * g, 2 * g + 1):
            normed.append(acc_ref[i, :LANES, :] * (1.0 / acc_ref[i, LANES:LANES + 1, :]))
        yt = normed[0] - lam * normed[1]
        ms = jnp.mean(yt * yt, axis=0, keepdims=True)
        yt = yt * lax.rsqrt(ms + NORM_EPS) * g_ref[...] * (1.0 - lam_init)
        o_ref[0, :, g * LANES:(g + 1) * LANES] = yt.T.astype(BF16)


def _diff_attention(qt, qkv, vt, bias_tiles, lq1, lk1, lq2, lk2, g_subln, b, s, lam_init):
    n_blk = DIFF_HEADS // GROUPS
    first = 3 * SB_HEADS * HEAD_DIM // (GROUPS * LANES)
    width = GROUPS * LANES
    n_chains = 2 * GROUPS
    vec = pl.BlockSpec((1, HEAD_DIM), lambda bi, h, qi: (0, 0))
    return pl.pallas_call(
        functools.partial(_diff_kernel, lam_init=lam_init),
        grid=(b, n_blk, s // TQ),
        in_specs=[
            pl.BlockSpec((1, 1, width, TQ), lambda bi, h, qi: (bi, qi, h, 0)),
            pl.BlockSpec((1, s, width), lambda bi, h, qi: (bi, 0, first + n_blk + h)),
            pl.BlockSpec((1, s // TK, GROUPS * VT_ROWS, TK), lambda bi, h, qi: (bi, 0, h, 0)),
            pl.BlockSpec((GROUPS, 4, TK, TQ), lambda bi, h, qi: (h, 0, 0, 0)),
            vec, vec, vec, vec,
            pl.BlockSpec((LANES, 1), lambda bi, h, qi: (0, 0)),
        ],
        out_specs=pl.BlockSpec((1, TQ, width), lambda bi, h, qi: (bi, qi, h)),
        out_shape=jax.ShapeDtypeStruct((b, s, DIFF_HEADS * LANES), BF16),
        scratch_shapes=[
            pltpu.VMEM((n_chains, TK, TQ), F32), pltpu.VMEM((n_chains, TK, TQ), F32),
            pltpu.VMEM((n_chains, TK, TQ), BF16), pltpu.VMEM((n_chains, TK, TQ), BF16),
            pltpu.VMEM((n_chains, 1, TQ), F32),
            pltpu.VMEM((n_chains, 1, TQ), F32), pltpu.VMEM((n_chains, 1, TQ), F32),
            pltpu.VMEM((n_chains, VT_ROWS, TQ), F32),
        ],
        compiler_params=pltpu.CompilerParams(
            dimension_semantics=("parallel", "parallel", "arbitrary"),
            vmem_limit_bytes=VMEM_LIMIT),
        name="diff_attention",
    )(qt, qkv, vt, bias_tiles, lq1, lk1, lq2, lk2, g_subln)


def _t5_bucket(rel):
    half = REL_BUCKETS // 2
    max_exact = half // 2
    ret = jnp.where(rel > 0, half, 0)
    n = jnp.abs(rel)
    nf = jnp.maximum(n, 1).astype(F32)
    large = max_exact + (jnp.log(nf / max_exact) / math.log(REL_MAX_DIST / max_exact)
                         * (half - max_exact)).astype(jnp.int32)
    large = jnp.minimum(large, half - 1)
    return ret + jnp.where(n < max_exact, n, large)


def _toeplitz(vals, first):
    period = vals.shape[1]
    rolled = jnp.roll(vals, -first, axis=1)
    skew = jnp.tile(rolled, (1, TQ))[:, :TQ * (period - 1)].reshape(-1, TQ, period - 1)
    return skew[:, :, :TK]


def _bias_tiles(rel_bias):
    rel = jnp.arange(-2 * TQ + 1, TK + 1, dtype=jnp.int32)
    by_rel = rel_bias.astype(F32)[_t5_bucket(rel)].T
    key = jnp.arange(TK)[:, None]
    query = jnp.arange(TQ)[None, :]
    allowed = key < (query // CHUNK + 1) * CHUNK
    diag = jnp.where(allowed, jnp.swapaxes(_toeplitz(by_rel, 2 * TQ - 1), 1, 2), MASKED)
    near = jnp.swapaxes(_toeplitz(by_rel, 2 * TQ - 1 - TK), 1, 2)
    far = jnp.broadcast_to(by_rel[:, :1, None], near.shape)
    dead = jnp.full(near.shape, MASKED, F32)
    return jnp.stack([diag, near, far, dead], axis=1)


def _feature_major(x, b, s, tile):
    return jnp.swapaxes(x.reshape(b, s // tile, tile, x.shape[-1]), 2, 3)


def _merge_kernel(x_ref, ysb_ref, ydf_ref, gpre_ref, wg_ref, wsb_ref, wdf_ref, wo_ref,
                  gpost_ref, o_ref):
    d = x_ref.shape[1]
    blocks = _row_blocks(x_ref.shape[0])
    gpre = gpre_ref[...]
    h = _rms(x_ref[blocks[0], :], gpre).astype(BF16)
    for k, rows in enumerate(blocks):
        h_now = h
        if k + 1 < len(blocks):
            h = _rms(x_ref[blocks[k + 1], :], gpre).astype(BF16)
        gates = jax.nn.sigmoid(_dot(h_now, wg_ref[...]))
        merged = (gates[:, :d] * _dot(ysb_ref[rows, :], wsb_ref[...])
                  + gates[:, d:] * _dot(ydf_ref[rows, :], wdf_ref[...]))
        o = _dot(merged.astype(BF16), wo_ref[...])
        o_ref[rows, :] = x_ref[rows, :] + _rms(o, gpost_ref[...])


def _merge(x2, ysb, ydf, gpre, wg, wsb, wdf, wo, gpost, tm):
    m, d = x2.shape

    def full(a):
        return pl.BlockSpec(a.shape, lambda i: (0, 0))

    def rows(a):
        return pl.BlockSpec((tm, a.shape[1]), lambda i: (i, 0))

    return pl.pallas_call(
        _merge_kernel,
        grid=(m // tm,),
        in_specs=[rows(x2), rows(ysb), rows(ydf), full(gpre), full(wg), full(wsb), full(wdf),
                  full(wo), full(gpost)],
        out_specs=pl.BlockSpec((tm, d), lambda i: (i, 0)),
        out_shape=jax.ShapeDtypeStruct((m, d), F32),
        compiler_params=pltpu.CompilerParams(
            dimension_semantics=("parallel",),
            vmem_limit_bytes=VMEM_LIMIT),
        name="merge",
    )(x2, ysb, ydf, gpre, wg, wsb, wdf, wo, gpost)


def _mlp_kernel(x_ref, gpre_ref, wup_ref, wdn_ref, gpost_ref, o_ref):
    blocks = _row_blocks(x_ref.shape[0])
    gpre = gpre_ref[...]
    h = _rms(x_ref[blocks[0], :], gpre).astype(BF16)
    for k, rows in enumerate(blocks):
        h_now = h
        if k + 1 < len(blocks):
            h = _rms(x_ref[blocks[k + 1], :], gpre).astype(BF16)
        acc = None
        for c in range(0, wup_ref.shape[1], COL_BLOCK):
            u = jnp.square(jnp.maximum(_dot(h_now, wup_ref[:, c:c + COL_BLOCK]), 0.0))
            part = _dot(u.astype(BF16), wdn_ref[c:c + COL_BLOCK, :])
            acc = part if acc is None else acc + part
        o_ref[rows, :] = x_ref[rows, :] + _rms(acc, gpost_ref[...])


def _mlp(x2, gpre, wup, wdn, gpost, tm):
    m, d = x2.shape

    def full(a):
        return pl.BlockSpec(a.shape, lambda i: (0, 0))

    return pl.pallas_call(
        _mlp_kernel,
        grid=(m // tm,),
        in_specs=[pl.BlockSpec((tm, d), lambda i: (i, 0)), full(gpre), full(wup), full(wdn),
                  full(gpost)],
        out_specs=pl.BlockSpec((tm, d), lambda i: (i, 0)),
        out_shape=jax.ShapeDtypeStruct((m, d), F32),
        compiler_params=pltpu.CompilerParams(
            dimension_semantics=("parallel",),
            vmem_limit_bytes=VMEM_LIMIT),
        name="mlp",
    )(x2, gpre, wup, wdn, gpost)


def kernel(x, w_in, w_sb_out, w_diff_out, w_o, lambda_q1, lambda_k1, lambda_q2, lambda_k2, w_subln, rel_bias, g_pre_mix, g_post_mix, g_pre_mlp, g_post_mlp, w_up, w_down):
    b, s, d = x.shape
    depth = w_in.shape[0]
    sb_width = SB_HEADS * HEAD_DIM
    diff_width = DIFF_HEADS * LANES
    n_qkv = 3 * sb_width + 3 * diff_width
    scale = HEAD_DIM ** -0.5
    col_scale = jnp.ones((n_qkv,), F32)
    col_scale = col_scale.at[:sb_width].set(scale)
    col_scale = col_scale.at[3 * sb_width:3 * sb_width + diff_width].set(scale)
    bias_tiles = _bias_tiles(rel_bias)

    x2 = x.reshape(b * s, d)
    for l in range(depth):
        lam_init = 0.8 - 0.6 * math.exp(-0.3 * l)
        w_qkv = (w_in[l][:, :n_qkv] * col_scale).astype(BF16)
        w_gate = w_in[l][:, n_qkv:].astype(BF16)
        qkv = _proj(x2, g_pre_mix[l][None], w_qkv, 512).reshape(b, s, n_qkv)
        y_sb = _sb_attention(qkv, b, s)
        d_q = qkv[..., 3 * sb_width:3 * sb_width + diff_width]
        d_v = qkv[..., 3 * sb_width + 2 * diff_width:]
        v_t = _feature_major(d_v, b, s, TK).reshape(b, s // TK, DIFF_HEADS, LANES, TK)
        v_t = jnp.concatenate(
            [v_t, jnp.ones((b, s // TK, DIFF_HEADS, ONES_ROWS, TK), BF16)], axis=3)
        y_diff = _diff_attention(_feature_major(d_q, b, s, TQ), qkv,
                                 v_t.reshape(b, s // TK, DIFF_HEADS * VT_ROWS, TK),
                                 bias_tiles, lambda_q1[l][None], lambda_k1[l][None],
                                 lambda_q2[l][None], lambda_k2[l][None], w_subln[l][:, None],
                                 b, s, lam_init)
        x2 = _merge(x2, y_sb.reshape(b * s, sb_width), y_diff.reshape(b * s, diff_width),
                    g_pre_mix[l][None], w_gate, w_sb_out[l].astype(BF16),
                    w_diff_out[l].astype(BF16), w_o[l].astype(BF16), g_post_mix[l][None], 512)
        x2 = _mlp(x2, g_pre_mlp[l][None], w_up[l].astype(BF16), w_down[l].astype(BF16),
                  g_post_mlp[l][None], 512)
    return x2.reshape(b, s, d)
```

```python
import functools
import math

import jax
import jax.numpy as jnp
from jax import lax
from jax.experimental import pallas as pl
from jax.experimental.pallas import tpu as pltpu

F32 = jnp.float32
BF16 = jnp.bfloat16

NORM_EPS = 1e-6
CHUNK = 64
SB_HEADS = 8
HEAD_DIM = 64
DIFF_HEADS = 4
LANES = 128
REL_BUCKETS = 32
REL_MAX_DIST = 128
MASKED = -1e30
LOG2E = 1.4426950408889634
DEAD_LOG2 = -160.0

VMEM_LIMIT = 56 * 1024 * 1024

TQ = 256
TK = 256
GROUPS = 2


def _rms(xf, g):
    ms = jnp.mean(xf * xf, axis=-1, keepdims=True)
    return xf * lax.rsqrt(ms + NORM_EPS) * g


def _dot(a, b):
    return jnp.dot(a, b, preferred_element_type=F32)


def _dot_nt(a, b):
    return lax.dot_general(a, b, (((1,), (1,)), ((), ())), preferred_element_type=F32)


def _half_lane_queries(q_ref):
    lane = lax.broadcasted_iota(jnp.int32, (TQ, LANES), 1)
    qs = []
    for g in range(GROUPS):
        q_all = q_ref[0, :, g * LANES:(g + 1) * LANES]
        for half in range(2):
            keep = (lane >= half * HEAD_DIM) & (lane < (half + 1) * HEAD_DIM)
            qs.append(jnp.where(keep, q_all, jnp.zeros_like(q_all)))
    return qs


def _key_tile(ref, kb, g):
    start = pl.multiple_of(kb * TK, TK)
    return ref[0, pl.ds(start, TK), g * LANES:(g + 1) * LANES]


ROW_BLOCK = 256
COL_BLOCK = 1024


def _row_blocks(tm):
    return [slice(r, r + ROW_BLOCK) for r in range(0, tm, ROW_BLOCK)]


def _proj_kernel(x_ref, g_ref, w_ref, o_ref, qt_ref, vt_ref, *, q_cols, v_cols):
    blocks = _row_blocks(x_ref.shape[0])
    g = g_ref[...]
    h = _rms(x_ref[blocks[0], :], g).astype(BF16)
    for k, rows in enumerate(blocks):
        h_now = h
        if k + 1 < len(blocks):
            h = _rms(x_ref[blocks[k + 1], :], g).astype(BF16)
        for c in range(0, w_ref.shape[1], COL_BLOCK):
            res = _dot(h_now, w_ref[:, c:c + COL_BLOCK])
            o_ref[rows, c:c + COL_BLOCK] = res.astype(BF16)
            if c <= q_cols[0] and q_cols[1] <= c + COL_BLOCK:
                qt_ref[0, k] = res[:, q_cols[0] - c:q_cols[1] - c].T.astype(BF16)
            if c <= v_cols[0] and v_cols[1] <= c + COL_BLOCK:
                v_t = res[:, v_cols[0] - c:v_cols[1] - c].T.astype(BF16)
                for head in range(DIFF_HEADS):
                    top = head * VT_ROWS
                    vt_ref[0, k, top:top + LANES, :] = v_t[head * LANES:(head + 1) * LANES, :]
                    vt_ref[0, k, top + LANES:top + VT_ROWS, :] = jnp.ones((ONES_ROWS, TK), BF16)


def _proj(x2, g, w, tm, b, s, q_cols, v_cols):
    m, d = x2.shape
    n = w.shape[1]
    per_batch = s // tm
    tiles = tm // ROW_BLOCK
    return pl.pallas_call(
        functools.partial(_proj_kernel, q_cols=q_cols, v_cols=v_cols),
        grid=(m // tm,),
        in_specs=[
            pl.BlockSpec((tm, d), lambda i: (i, 0)),
            pl.BlockSpec((1, d), lambda i: (0, 0)),
            pl.BlockSpec((d, n), lambda i: (0, 0)),
        ],
        out_specs=[
            pl.BlockSpec((tm, n), lambda i: (i, 0)),
            pl.BlockSpec((1, tiles, q_cols[1] - q_cols[0], TQ),
                         lambda i: (i // per_batch, i % per_batch, 0, 0)),
            pl.BlockSpec((1, tiles, DIFF_HEADS * VT_ROWS, TK),
                         lambda i: (i // per_batch, i % per_batch, 0, 0)),
        ],
        out_shape=[
            jax.ShapeDtypeStruct((m, n), BF16),
            jax.ShapeDtypeStruct((b, s // TQ, q_cols[1] - q_cols[0], TQ), BF16),
            jax.ShapeDtypeStruct((b, s // TK, DIFF_HEADS * VT_ROWS, TK), BF16),
        ],
        compiler_params=pltpu.CompilerParams(
            dimension_semantics=("parallel",),
            vmem_limit_bytes=VMEM_LIMIT),
        name="proj",
    )(x2, g, w)


def _sb_scores(qs, ks):
    return [_dot_nt(q, k) * LOG2E for q, k in zip(qs, ks)]


def _sb_weights(zs, vs, upper, states, mask):
    n = len(zs)
    log_betas, log_oms, his, los = [], [], [], []
    for z in zs:
        soft = jnp.log2(1.0 + jnp.exp2(-jnp.abs(z)))
        log_beta = jnp.minimum(z, 0.0) - soft
        log_om = log_beta - z
        if mask is not None:
            log_om = jnp.where(mask, log_om, 0.0)
        hi = log_om.astype(BF16)
        log_betas.append(log_beta)
        log_oms.append(log_om)
        his.append(hi)
        los.append((log_om - hi.astype(F32)).astype(BF16))
    tails = [_dot(his[i], upper) + _dot(los[i], upper) for i in range(n)]
    ws = []
    for i in range(n):
        w = jnp.exp2(log_betas[i] + tails[i])
        if mask is not None:
            w = jnp.where(mask, w, 0.0)
        ws.append(w.astype(BF16))
    new = []
    for i in range(n):
        c, acc = states[i]
        acc = acc + jnp.exp2(c) * _dot(ws[i], vs[i])
        c = c + jnp.sum(log_oms[i], axis=1, keepdims=True)
        new.append((c, acc))
    return tuple(new)


def _sb_kernel(q_ref, k_ref, v_ref, o_ref):
    qi = pl.program_id(2)
    row = lax.broadcasted_iota(jnp.int32, (TQ, TK), 0)
    col = lax.broadcasted_iota(jnp.int32, (TQ, TK), 1)
    causal = col < row
    upper = jnp.where(row > col, 1.0, 0.0).astype(BF16)
    qs = _half_lane_queries(q_ref)
    n_chains = 2 * GROUPS

    def scores(kb):
        return _sb_scores(qs, [_key_tile(k_ref, kb, i // 2) for i in range(n_chains)])

    def weights(kb, zs, states, mask):
        vs = [_key_tile(v_ref, kb, i // 2) for i in range(n_chains)]
        return _sb_weights(zs, vs, upper, states, mask)

    def any_weight_left(states):
        c_max = functools.reduce(jnp.maximum, [jnp.max(st[0]) for st in states])
        return c_max > DEAD_LOG2

    def first_two(states):
        z_diag, z_next = scores(qi), scores(qi - 1)
        states = weights(qi, z_diag, states, causal)
        return weights(qi - 1, z_next, states, None)

    def step(carry):
        j, _, states = carry
        states = weights(qi - 1 - j, scores(qi - 1 - j), states, None)
        return j + 1, any_weight_left(states), states

    init = tuple((jnp.zeros((TQ, 1), F32), jnp.zeros((TQ, LANES), F32))
                 for _ in range(n_chains))
    states = lax.cond(qi >= 1, first_two,
                      lambda st: weights(qi, scores(qi), st, causal), init)
    _, _, states = lax.while_loop(lambda carry: (carry[0] < qi) & carry[1], step,
                                  (jnp.int32(1), any_weight_left(states), states))

    lane = lax.broadcasted_iota(jnp.int32, (TQ, LANES), 1)
    for g in range(GROUPS):
        out = jnp.where(lane < HEAD_DIM, states[2 * g][1], states[2 * g + 1][1])
        o_ref[0, :, g * LANES:(g + 1) * LANES] = out.astype(BF16)


def _sb_attention(qkv, b, s):
    n_blk = SB_HEADS * HEAD_DIM // (GROUPS * LANES)
    width = GROUPS * LANES
    return pl.pallas_call(
        _sb_kernel,
        grid=(b, n_blk, s // TQ),
        in_specs=[
            pl.BlockSpec((1, TQ, width), lambda bi, p, qi: (bi, qi, p)),
            pl.BlockSpec((1, s, width), lambda bi, p, qi: (bi, 0, n_blk + p)),
            pl.BlockSpec((1, s, width), lambda bi, p, qi: (bi, 0, 2 * n_blk + p)),
        ],
        out_specs=pl.BlockSpec((1, TQ, width), lambda bi, p, qi: (bi, qi, p)),
        out_shape=jax.ShapeDtypeStruct((b, s, SB_HEADS * HEAD_DIM), BF16),
        compiler_params=pltpu.CompilerParams(
            dimension_semantics=("parallel", "parallel", "arbitrary"),
            vmem_limit_bytes=VMEM_LIMIT),
        name="sb_attention",
    )(qkv, qkv, qkv)


BIAS_DIAG, BIAS_NEAR, BIAS_FAR, BIAS_DEAD = 0, 1, 2, 3
ONES_ROWS = 16
VT_ROWS = LANES + ONES_ROWS


def _diff_kernel(qt_ref, k_ref, vt_ref, bias_ref, lq1_ref, lk1_ref, lq2_ref, lk2_ref, g_ref,
                 o_ref, z_a, z_b, p_a, p_b, m_ref, alpha_a, alpha_b, acc_ref, *, lam_init):
    qi = pl.program_id(2)
    n_chains = 2 * GROUPS
    feature = lax.broadcasted_iota(jnp.int32, (LANES, TQ), 0)
    qts = []
    for g in range(GROUPS):
        qt_all = qt_ref[0, 0, g * LANES:(g + 1) * LANES, :]
        for half in range(2):
            keep = (feature >= half * HEAD_DIM) & (feature < (half + 1) * HEAD_DIM)
            qts.append(jnp.where(keep, qt_all, jnp.zeros_like(qt_all)))

    def key_block(j):
        return jnp.clip(qi - j, 0, qi)

    def bias_kind(j):
        return jnp.where(j > qi, BIAS_DEAD, jnp.minimum(j, BIAS_FAR))

    def scores(j, z_ref, chains=range(2 * GROUPS)):
        kb = key_block(j)
        for i in chains:
            z_ref[i] = _dot(_key_tile(k_ref, kb, i // 2), qts[i])

    def normalise(j, z_ref, p_ref, alpha_ref, per_element_bias, chains=range(2 * GROUPS)):
        kind = bias_kind(j)
        for i in chains:
            m_old = m_ref[i]
            if per_element_bias:
                bias = bias_ref.at[i // 2, kind]
                m_new = jnp.maximum(m_old, jnp.max(z_ref[i] + bias[...], axis=0, keepdims=True))
                m_ref[i] = m_new
                p = jnp.exp(z_ref[i] + bias[...] - m_new)
            else:
                c = bias_ref[i // 2, kind, 0:1, 0:1]
                m_new = jnp.maximum(m_old, jnp.max(z_ref[i], axis=0, keepdims=True) + c)
                m_ref[i] = m_new
                p = jnp.exp(z_ref[i] - (m_new - c))
            alpha_ref[i] = jnp.exp(m_old - m_new)
            p_ref[i] = p.astype(BF16)

    def accumulate(j, p_ref, alpha_ref, chains=range(2 * GROUPS)):
        kb = key_block(j)
        for i in chains:
            g = i // 2
            vt = vt_ref[0, kb, g * VT_ROWS:(g + 1) * VT_ROWS, :]
            acc_ref[i] = alpha_ref[i] * acc_ref[i] + _dot(vt, p_ref[i])

    m_ref[...] = jnp.full(m_ref.shape, MASKED, F32)
    acc_ref[...] = jnp.zeros(acc_ref.shape, F32)
    scores(0, z_a)
    normalise(0, z_a, p_a, alpha_a, True)
    scores(1, z_b)
    accumulate(0, p_a, alpha_a)
    normalise(1, z_b, p_b, alpha_b, True)
    scores(2, z_a)

    def trip(t, carry):
        j = 2 * t
        for i in range(n_chains):
            normalise(j, z_a, p_a, alpha_a, False, (i,))
            accumulate(j - 1, p_b, alpha_b, (i,))
            scores(j + 1, z_b, (i,))
        for i in range(n_chains):
            normalise(j + 1, z_b, p_b, alpha_b, False, (i,))
            accumulate(j, p_a, alpha_a, (i,))
            scores(j + 2, z_a, (i,))
        return carry

    n_trips = (qi + 2) // 2
    lax.fori_loop(1, n_trips, trip, 0)
    accumulate(2 * n_trips - 1, p_b, alpha_b)

    lam = (jnp.exp(jnp.sum(lq1_ref[...] * lk1_ref[...], axis=1, keepdims=True))
           - jnp.exp(jnp.sum(lq2_ref[...] * lk2_ref[...], axis=1, keepdims=True))
           + lam_init)
    for g in range(GROUPS):
        normed = []
        for i in (2 * g, 2 * g + 1):
            normed.append(acc_ref[i, :LANES, :] * (1.0 / acc_ref[i, LANES:LANES + 1, :]))
        yt = normed[0] - lam * normed[1]
        ms = jnp.mean(yt * yt, axis=0, keepdims=True)
        yt = yt * lax.rsqrt(ms + NORM_EPS) * g_ref[...] * (1.0 - lam_init)
        o_ref[0, :, g * LANES:(g + 1) * LANES] = yt.T.astype(BF16)


def _diff_attention(qt, qkv, vt, bias_tiles, lq1, lk1, lq2, lk2, g_subln, b, s, lam_init):
    n_blk = DIFF_HEADS // GROUPS
    first = 3 * SB_HEADS * HEAD_DIM // (GROUPS * LANES)
    width = GROUPS * LANES
    n_chains = 2 * GROUPS
    vec = pl.BlockSpec((1, HEAD_DIM), lambda bi, h, qi: (0, 0))
    return pl.pallas_call(
        functools.partial(_diff_kernel, lam_init=lam_init),
        grid=(b, n_blk, s // TQ),
        in_specs=[
            pl.BlockSpec((1, 1, width, TQ), lambda bi, h, qi: (bi, qi, h, 0)),
            pl.BlockSpec((1, s, width), lambda bi, h, qi: (bi, 0, first + n_blk + h)),
            pl.BlockSpec((1, s // TK, GROUPS * VT_ROWS, TK), lambda bi, h, qi: (bi, 0, h, 0)),
            pl.BlockSpec((GROUPS, 4, TK, TQ), lambda bi, h, qi: (h, 0, 0, 0)),
            vec, vec, vec, vec,
            pl.BlockSpec((LANES, 1), lambda bi, h, qi: (0, 0)),
        ],
        out_specs=pl.BlockSpec((1, TQ, width), lambda bi, h, qi: (bi, qi, h)),
        out_shape=jax.ShapeDtypeStruct((b, s, DIFF_HEADS * LANES), BF16),
        scratch_shapes=[
            pltpu.VMEM((n_chains, TK, TQ), F32), pltpu.VMEM((n_chains, TK, TQ), F32),
            pltpu.VMEM((n_chains, TK, TQ), BF16), pltpu.VMEM((n_chains, TK, TQ), BF16),
            pltpu.VMEM((n_chains, 1, TQ), F32),
            pltpu.VMEM((n_chains, 1, TQ), F32), pltpu.VMEM((n_chains, 1, TQ), F32),
            pltpu.VMEM((n_chains, VT_ROWS, TQ), F32),
        ],
        compiler_params=pltpu.CompilerParams(
            dimension_semantics=("parallel", "parallel", "arbitrary"),
            vmem_limit_bytes=VMEM_LIMIT),
        name="diff_attention",
    )(qt, qkv, vt, bias_tiles, lq1, lk1, lq2, lk2, g_subln)


def _t5_bucket(rel):
    half = REL_BUCKETS // 2
    max_exact = half // 2
    ret = jnp.where(rel > 0, half, 0)
    n = jnp.abs(rel)
    nf = jnp.maximum(n, 1).astype(F32)
    large = max_exact + (jnp.log(nf / max_exact) / math.log(REL_MAX_DIST / max_exact)
                         * (half - max_exact)).astype(jnp.int32)
    large = jnp.minimum(large, half - 1)
    return ret + jnp.where(n < max_exact, n, large)


def _toeplitz(vals, first):
    period = vals.shape[1]
    rolled = jnp.roll(vals, -first, axis=1)
    skew = jnp.tile(rolled, (1, TQ))[:, :TQ * (period - 1)].reshape(-1, TQ, period - 1)
    return skew[:, :, :TK]


def _bias_tiles(rel_bias):
    rel = jnp.arange(-2 * TQ + 1, TK + 1, dtype=jnp.int32)
    by_rel = rel_bias.astype(F32)[_t5_bucket(rel)].T
    key = jnp.arange(TK)[:, None]
    query = jnp.arange(TQ)[None, :]
    allowed = key < (query // CHUNK + 1) * CHUNK
    diag = jnp.where(allowed, jnp.swapaxes(_toeplitz(by_rel, 2 * TQ - 1), 1, 2), MASKED)
    near = jnp.swapaxes(_toeplitz(by_rel, 2 * TQ - 1 - TK), 1, 2)
    far = jnp.broadcast_to(by_rel[:, :1, None], near.shape)
    dead = jnp.full(near.shape, MASKED, F32)
    return jnp.stack([diag, near, far, dead], axis=1)


def _merge_kernel(x_ref, ysb_ref, ydf_ref, gpre_ref, wg_ref, wsb_ref, wdf_ref, wo_ref,
                  gpost_ref, o_ref):
    d = x_ref.shape[1]
    blocks = _row_blocks(x_ref.shape[0])
    gpre = gpre_ref[...]
    h = _rms(x_ref[blocks[0], :], gpre).astype(BF16)
    for k, rows in enumerate(blocks):
        h_now = h
        if k + 1 < len(blocks):
            h = _rms(x_ref[blocks[k + 1], :], gpre).astype(BF16)
        gates = jax.nn.sigmoid(_dot(h_now, wg_ref[...]))
        merged = (gates[:, :d] * _dot(ysb_ref[rows, :], wsb_ref[...])
                  + gates[:, d:] * _dot(ydf_ref[rows, :], wdf_ref[...]))
        o = _dot(merged.astype(BF16), wo_ref[...])
        o_ref[rows, :] = x_ref[rows, :] + _rms(o, gpost_ref[...])


def _merge(x2, ysb, ydf, gpre, wg, wsb, wdf, wo, gpost, tm):
    m, d = x2.shape

    def full(a):
        return pl.BlockSpec(a.shape, lambda i: (0, 0))

    def rows(a):
        return pl.BlockSpec((tm, a.shape[1]), lambda i: (i, 0))

    return pl.pallas_call(
        _merge_kernel,
        grid=(m // tm,),
        in_specs=[rows(x2), rows(ysb), rows(ydf), full(gpre), full(wg), full(wsb), full(wdf),
                  full(wo), full(gpost)],
        out_specs=pl.BlockSpec((tm, d), lambda i: (i, 0)),
        out_shape=jax.ShapeDtypeStruct((m, d), F32),
        compiler_params=pltpu.CompilerParams(
            dimension_semantics=("parallel",),
            vmem_limit_bytes=VMEM_LIMIT),
        name="merge",
    )(x2, ysb, ydf, gpre, wg, wsb, wdf, wo, gpost)


def _mlp_kernel(x_ref, gpre_ref, wup_ref, wdn_ref, gpost_ref, o_ref):
    blocks = _row_blocks(x_ref.shape[0])
    gpre = gpre_ref[...]
    h = _rms(x_ref[blocks[0], :], gpre).astype(BF16)
    for k, rows in enumerate(blocks):
        h_now = h
        if k + 1 < len(blocks):
            h = _rms(x_ref[blocks[k + 1], :], gpre).astype(BF16)
        acc = None
        for c in range(0, wup_ref.shape[1], COL_BLOCK):
            u = jnp.square(jnp.maximum(_dot(h_now, wup_ref[:, c:c + COL_BLOCK]), 0.0))
            part = _dot(u.astype(BF16), wdn_ref[c:c + COL_BLOCK, :])
            acc = part if acc is None else acc + part
        o_ref[rows, :] = x_ref[rows, :] + _rms(acc, gpost_ref[...])


def _mlp(x2, gpre, wup, wdn, gpost, tm):
    m, d = x2.shape

    def full(a):
        return pl.BlockSpec(a.shape, lambda i: (0, 0))

    return pl.pallas_call(
        _mlp_kernel,
        grid=(m // tm,),
        in_specs=[pl.BlockSpec((tm, d), lambda i: (i, 0)), full(gpre), full(wup), full(wdn),
                  full(gpost)],
        out_specs=pl.BlockSpec((tm, d), lambda i: (i, 0)),
        out_shape=jax.ShapeDtypeStruct((m, d), F32),
        compiler_params=pltpu.CompilerParams(
            dimension_semantics=("parallel",),
            vmem_limit_bytes=VMEM_LIMIT),
        name="mlp",
    )(x2, gpre, wup, wdn, gpost)


def kernel(x, w_in, w_sb_out, w_diff_out, w_o, lambda_q1, lambda_k1, lambda_q2, lambda_k2, w_subln, rel_bias, g_pre_mix, g_post_mix, g_pre_mlp, g_post_mlp, w_up, w_down):
    b, s, d = x.shape
    depth = w_in.shape[0]
    sb_width = SB_HEADS * HEAD_DIM
    diff_width = DIFF_HEADS * LANES
    n_qkv = 3 * sb_width + 3 * diff_width
    scale = HEAD_DIM ** -0.5
    col_scale = jnp.ones((n_qkv,), F32)
    col_scale = col_scale.at[:sb_width].set(scale)
    col_scale = col_scale.at[3 * sb_width:3 * sb_width + diff_width].set(scale)
    bias_tiles = _bias_tiles(rel_bias)

    x2 = x.reshape(b * s, d)
    for l in range(depth):
        lam_init = 0.8 - 0.6 * math.exp(-0.3 * l)
        w_qkv = (w_in[l][:, :n_qkv] * col_scale).astype(BF16)
        w_gate = w_in[l][:, n_qkv:].astype(BF16)
        d_q0 = 3 * sb_width
        d_v0 = d_q0 + 2 * diff_width
        qkv, q_t, v_t = _proj(x2, g_pre_mix[l][None], w_qkv, 512, b, s,
                              (d_q0, d_q0 + diff_width), (d_v0, d_v0 + diff_width))
        qkv = qkv.reshape(b, s, n_qkv)
        y_sb = _sb_attention(qkv, b, s)
        y_diff = _diff_attention(q_t, qkv, v_t, bias_tiles, lambda_q1[l][None],
                                 lambda_k1[l][None], lambda_q2[l][None], lambda_k2[l][None],
                                 w_subln[l][:, None], b, s, lam_init)
        x2 = _merge(x2, y_sb.reshape(b * s, sb_width), y_diff.reshape(b * s, diff_width),
                    g_pre_mix[l][None], w_gate, w_sb_out[l].astype(BF16),
                    w_diff_out[l].astype(BF16), w_o[l].astype(BF16), g_post_mix[l][None], 512)
        x2 = _mlp(x2, g_pre_mlp[l][None], w_up[l].astype(BF16), w_down[l].astype(BF16),
                  g_post_mlp[l][None], 512)
    return x2.reshape(b, s, d)
```

```python
import functools
import math

import jax
import jax.numpy as jnp
from jax import lax
from jax.experimental import pallas as pl
from jax.experimental.pallas import tpu as pltpu

F32 = jnp.float32
BF16 = jnp.bfloat16

NORM_EPS = 1e-6
CHUNK = 64
SB_HEADS = 8
HEAD_DIM = 64
DIFF_HEADS = 4
LANES = 128
REL_BUCKETS = 32
REL_MAX_DIST = 128
MASKED = -1e30
LOG2E = 1.4426950408889634
DEAD_LOG = -111.0

VMEM_LIMIT = 56 * 1024 * 1024

TQ = 256
TK = 256
GROUPS = 2
DIFF_GROUPS = 4


def _rms(xf, g):
    ms = jnp.mean(xf * xf, axis=-1, keepdims=True)
    return xf * lax.rsqrt(ms + NORM_EPS) * g


def _dot(a, b):
    return jnp.dot(a, b, preferred_element_type=F32)


def _dot_nt(a, b):
    return lax.dot_general(a, b, (((1,), (1,)), ((), ())), preferred_element_type=F32)


def _half_lane_queries(q_ref):
    lane = lax.broadcasted_iota(jnp.int32, (TQ, LANES), 1)
    qs = []
    for g in range(GROUPS):
        q_all = q_ref[0, :, g * LANES:(g + 1) * LANES]
        for half in range(2):
            keep = (lane >= half * HEAD_DIM) & (lane < (half + 1) * HEAD_DIM)
            qs.append(jnp.where(keep, q_all, jnp.zeros_like(q_all)))
    return qs


def _key_tile(ref, kb, g):
    start = pl.multiple_of(kb * TK, TK)
    return ref[0, pl.ds(start, TK), g * LANES:(g + 1) * LANES]


ROW_BLOCK = 256
COL_BLOCK = 1024


def _row_blocks(tm):
    return [slice(r, r + ROW_BLOCK) for r in range(0, tm, ROW_BLOCK)]


def _proj_kernel(x_ref, g_ref, w_ref, o_ref, qt_ref, vt_ref, *, q_cols, v_cols):
    blocks = _row_blocks(x_ref.shape[0])
    g = g_ref[...]
    h = _rms(x_ref[blocks[0], :], g).astype(BF16)
    for k, rows in enumerate(blocks):
        h_now = h
        if k + 1 < len(blocks):
            h = _rms(x_ref[blocks[k + 1], :], g).astype(BF16)
        for c in range(0, w_ref.shape[1], COL_BLOCK):
            res = _dot(h_now, w_ref[:, c:c + COL_BLOCK])
            o_ref[rows, c:c + COL_BLOCK] = res.astype(BF16)
            if c <= q_cols[0] and q_cols[1] <= c + COL_BLOCK:
                qt_ref[0, k] = res[:, q_cols[0] - c:q_cols[1] - c].T.astype(BF16)
            if c <= v_cols[0] and v_cols[1] <= c + COL_BLOCK:
                v_t = res[:, v_cols[0] - c:v_cols[1] - c].T.astype(BF16)
                for head in range(DIFF_HEADS):
                    top = head * VT_ROWS
                    vt_ref[0, k, top:top + LANES, :] = v_t[head * LANES:(head + 1) * LANES, :]
                    vt_ref[0, k, top + LANES:top + VT_ROWS, :] = jnp.ones((ONES_ROWS, TK), BF16)


def _proj(x2, g, w, tm, b, s, q_cols, v_cols):
    m, d = x2.shape
    n = w.shape[1]
    per_batch = s // tm
    tiles = tm // ROW_BLOCK
    return pl.pallas_call(
        functools.partial(_proj_kernel, q_cols=q_cols, v_cols=v_cols),
        grid=(m // tm,),
        in_specs=[
            pl.BlockSpec((tm, d), lambda i: (i, 0)),
            pl.BlockSpec((1, d), lambda i: (0, 0)),
            pl.BlockSpec((d, n), lambda i: (0, 0)),
        ],
        out_specs=[
            pl.BlockSpec((tm, n), lambda i: (i, 0)),
            pl.BlockSpec((1, tiles, q_cols[1] - q_cols[0], TQ),
                         lambda i: (i // per_batch, i % per_batch, 0, 0)),
            pl.BlockSpec((1, tiles, DIFF_HEADS * VT_ROWS, TK),
                         lambda i: (i // per_batch, i % per_batch, 0, 0)),
        ],
        out_shape=[
            jax.ShapeDtypeStruct((m, n), BF16),
            jax.ShapeDtypeStruct((b, s // TQ, q_cols[1] - q_cols[0], TQ), BF16),
            jax.ShapeDtypeStruct((b, s // TK, DIFF_HEADS * VT_ROWS, TK), BF16),
        ],
        compiler_params=pltpu.CompilerParams(
            dimension_semantics=("parallel",),
            vmem_limit_bytes=VMEM_LIMIT),
        name="proj",
    )(x2, g, w)


def _sb_scores(qs, ks):
    return [_dot_nt(q, k) for q, k in zip(qs, ks)]


def _sb_weights(zs, vs, upper, states, mask):
    n = len(zs)
    log_betas, log_oms, his, los = [], [], [], []
    for z in zs:
        soft = jnp.log(1.0 + jnp.exp2(jnp.abs(z) * -LOG2E))
        log_beta = jnp.minimum(z, 0.0) - soft
        log_om = log_beta - z
        if mask is not None:
            log_om = jnp.where(mask, log_om, 0.0)
        hi = log_om.astype(BF16)
        log_betas.append(log_beta)
        log_oms.append(log_om)
        his.append(hi)
        los.append((log_om - hi.astype(F32)).astype(BF16))
    tails = [_dot(his[i], upper) + _dot(los[i], upper) for i in range(n)]
    ws = []
    for i in range(n):
        w = jnp.exp(log_betas[i] + tails[i])
        if mask is not None:
            w = jnp.where(mask, w, 0.0)
        ws.append(w.astype(BF16))
    new = []
    for i in range(n):
        c, acc = states[i]
        acc = acc + jnp.exp(c) * _dot(ws[i], vs[i])
        c = c + jnp.sum(log_oms[i], axis=1, keepdims=True)
        new.append((c, acc))
    return tuple(new)


def _sb_kernel(q_ref, k_ref, v_ref, o_ref):
    qi = pl.program_id(2)
    row = lax.broadcasted_iota(jnp.int32, (TQ, TK), 0)
    col = lax.broadcasted_iota(jnp.int32, (TQ, TK), 1)
    causal = col < row
    upper = jnp.where(row > col, 1.0, 0.0).astype(BF16)
    qs = _half_lane_queries(q_ref)
    n_chains = 2 * GROUPS

    def scores(kb):
        return _sb_scores(qs, [_key_tile(k_ref, kb, i // 2) for i in range(n_chains)])

    def weights(kb, zs, states, mask):
        vs = [_key_tile(v_ref, kb, i // 2) for i in range(n_chains)]
        return _sb_weights(zs, vs, upper, states, mask)

    def any_weight_left(states):
        c_max = functools.reduce(jnp.maximum, [jnp.max(st[0]) for st in states])
        return c_max > DEAD_LOG

    def first_two(states):
        z_diag, z_next = scores(qi), scores(qi - 1)
        states = weights(qi, z_diag, states, causal)
        return weights(qi - 1, z_next, states, None)

    def step(carry):
        j, _, states = carry
        states = weights(qi - 1 - j, scores(qi - 1 - j), states, None)
        return j + 1, any_weight_left(states), states

    init = tuple((jnp.zeros((TQ, 1), F32), jnp.zeros((TQ, LANES), F32))
                 for _ in range(n_chains))
    states = lax.cond(qi >= 1, first_two,
                      lambda st: weights(qi, scores(qi), st, causal), init)
    _, _, states = lax.while_loop(lambda carry: (carry[0] < qi) & carry[1], step,
                                  (jnp.int32(1), any_weight_left(states), states))

    lane = lax.broadcasted_iota(jnp.int32, (TQ, LANES), 1)
    for g in range(GROUPS):
        out = jnp.where(lane < HEAD_DIM, states[2 * g][1], states[2 * g + 1][1])
        o_ref[0, :, g * LANES:(g + 1) * LANES] = out.astype(BF16)


def _sb_attention(qkv, b, s):
    n_blk = SB_HEADS * HEAD_DIM // (GROUPS * LANES)
    width = GROUPS * LANES
    return pl.pallas_call(
        _sb_kernel,
        grid=(b, n_blk, s // TQ),
        in_specs=[
            pl.BlockSpec((1, TQ, width), lambda bi, p, qi: (bi, qi, p)),
            pl.BlockSpec((1, s, width), lambda bi, p, qi: (bi, 0, n_blk + p)),
            pl.BlockSpec((1, s, width), lambda bi, p, qi: (bi, 0, 2 * n_blk + p)),
        ],
        out_specs=pl.BlockSpec((1, TQ, width), lambda bi, p, qi: (bi, qi, p)),
        out_shape=jax.ShapeDtypeStruct((b, s, SB_HEADS * HEAD_DIM), BF16),
        compiler_params=pltpu.CompilerParams(
            dimension_semantics=("parallel", "parallel", "arbitrary"),
            vmem_limit_bytes=VMEM_LIMIT),
        name="sb_attention",
    )(qkv, qkv, qkv)


BIAS_DIAG, BIAS_NEAR, BIAS_FAR, BIAS_DEAD = 0, 1, 2, 3
ONES_ROWS = 16
VT_ROWS = LANES + ONES_ROWS


def _diff_kernel(qt_ref, k_ref, vt_ref, bias_ref, lq1_ref, lk1_ref, lq2_ref, lk2_ref, g_ref,
                 o_ref, z_a, z_b, p_a, p_b, m_ref, alpha_a, alpha_b, acc_ref, *, lam_init):
    qi = pl.program_id(2)
    n_chains = 2 * DIFF_GROUPS
    feature = lax.broadcasted_iota(jnp.int32, (LANES, TQ), 0)
    qts = []
    for g in range(DIFF_GROUPS):
        qt_all = qt_ref[0, 0, g * LANES:(g + 1) * LANES, :]
        for half in range(2):
            keep = (feature >= half * HEAD_DIM) & (feature < (half + 1) * HEAD_DIM)
            qts.append(jnp.where(keep, qt_all, jnp.zeros_like(qt_all)))

    def key_block(j):
        return jnp.clip(qi - j, 0, qi)

    def bias_kind(j):
        return jnp.where(j > qi, BIAS_DEAD, jnp.minimum(j, BIAS_FAR))

    def scores(j, z_ref, chains=range(2 * DIFF_GROUPS)):
        kb = key_block(j)
        for i in chains:
            z_ref[i] = _dot(_key_tile(k_ref, kb, i // 2), qts[i])

    def normalise(j, z_ref, p_ref, alpha_ref, per_element_bias, chains=range(2 * DIFF_GROUPS)):
        kind = bias_kind(j)
        for i in chains:
            m_old = m_ref[i]
            if per_element_bias:
                bias = bias_ref.at[i // 2, kind]
                m_new = jnp.maximum(m_old, jnp.max(z_ref[i] + bias[...], axis=0, keepdims=True))
                m_ref[i] = m_new
                p = jnp.exp(z_ref[i] + bias[...] - m_new)
            else:
                c = bias_ref[i // 2, kind, 0:1, 0:1]
                m_new = jnp.maximum(m_old, jnp.max(z_ref[i], axis=0, keepdims=True) + c)
                m_ref[i] = m_new
                p = jnp.exp(z_ref[i] - (m_new - c))
            alpha_ref[i] = jnp.exp(m_old - m_new)
            p_ref[i] = p.astype(BF16)

    def accumulate(j, p_ref, alpha_ref, chains=range(2 * DIFF_GROUPS)):
        kb = key_block(j)
        for i in chains:
            g = i // 2
            vt = vt_ref[0, kb, g * VT_ROWS:(g + 1) * VT_ROWS, :]
            acc_ref[i] = alpha_ref[i] * acc_ref[i] + _dot(vt, p_ref[i])

    m_ref[...] = jnp.full(m_ref.shape, MASKED, F32)
    acc_ref[...] = jnp.zeros(acc_ref.shape, F32)
    scores(0, z_a)
    normalise(0, z_a, p_a, alpha_a, True)
    scores(1, z_b)
    accumulate(0, p_a, alpha_a)
    normalise(1, z_b, p_b, alpha_b, True)
    scores(2, z_a)

    def trip(t, carry):
        j = 2 * t
        for i in range(n_chains):
            normalise(j, z_a, p_a, alpha_a, False, (i,))
            accumulate(j - 1, p_b, alpha_b, (i,))
            scores(j + 1, z_b, (i,))
        for i in range(n_chains):
            normalise(j + 1, z_b, p_b, alpha_b, False, (i,))
            accumulate(j, p_a, alpha_a, (i,))
            scores(j + 2, z_a, (i,))
        return carry

    n_trips = (qi + 2) // 2
    lax.fori_loop(1, n_trips, trip, 0)
    accumulate(2 * n_trips - 1, p_b, alpha_b)

    lam = (jnp.exp(jnp.sum(lq1_ref[...] * lk1_ref[...], axis=1, keepdims=True))
           - jnp.exp(jnp.sum(lq2_ref[...] * lk2_ref[...], axis=1, keepdims=True))
           + lam_init)
    for g in range(DIFF_GROUPS):
        normed = []
        for i in (2 * g, 2 * g + 1):
            normed.append(acc_ref[i, :LANES, :] * (1.0 / acc_ref[i, LANES:LANES + 1, :]))
        yt = normed[0] - lam * normed[1]
        ms = jnp.mean(yt * yt, axis=0, keepdims=True)
        yt = yt * lax.rsqrt(ms + NORM_EPS) * g_ref[...] * (1.0 - lam_init)
        o_ref[0, :, g * LANES:(g + 1) * LANES] = yt.T.astype(BF16)


def _diff_attention(qt, qkv, vt, bias_tiles, lq1, lk1, lq2, lk2, g_subln, b, s, lam_init):
    n_blk = DIFF_HEADS // DIFF_GROUPS
    first = 3 * SB_HEADS * HEAD_DIM // (DIFF_GROUPS * LANES)
    width = DIFF_GROUPS * LANES
    n_chains = 2 * DIFF_GROUPS
    vec = pl.BlockSpec((1, HEAD_DIM), lambda bi, h, qi: (0, 0))
    return pl.pallas_call(
        functools.partial(_diff_kernel, lam_init=lam_init),
        grid=(b, n_blk, s // TQ),
        in_specs=[
            pl.BlockSpec((1, 1, width, TQ), lambda bi, h, qi: (bi, qi, h, 0)),
            pl.BlockSpec((1, s, width), lambda bi, h, qi: (bi, 0, first + n_blk + h)),
            pl.BlockSpec((1, s // TK, DIFF_GROUPS * VT_ROWS, TK), lambda bi, h, qi: (bi, 0, h, 0)),
            pl.BlockSpec((DIFF_GROUPS, 4, TK, TQ), lambda bi, h, qi: (h, 0, 0, 0)),
            vec, vec, vec, vec,
            pl.BlockSpec((LANES, 1), lambda bi, h, qi: (0, 0)),
        ],
        out_specs=pl.BlockSpec((1, TQ, width), lambda bi, h, qi: (bi, qi, h)),
        out_shape=jax.ShapeDtypeStruct((b, s, DIFF_HEADS * LANES), BF16),
        scratch_shapes=[
            pltpu.VMEM((n_chains, TK, TQ), F32), pltpu.VMEM((n_chains, TK, TQ), F32),
            pltpu.VMEM((n_chains, TK, TQ), BF16), pltpu.VMEM((n_chains, TK, TQ), BF16),
            pltpu.VMEM((n_chains, 1, TQ), F32),
            pltpu.VMEM((n_chains, 1, TQ), F32), pltpu.VMEM((n_chains, 1, TQ), F32),
            pltpu.VMEM((n_chains, VT_ROWS, TQ), F32),
        ],
        compiler_params=pltpu.CompilerParams(
            dimension_semantics=("parallel", "parallel", "arbitrary"),
            vmem_limit_bytes=VMEM_LIMIT),
        name="diff_attention",
    )(qt, qkv, vt, bias_tiles, lq1, lk1, lq2, lk2, g_subln)


def _t5_bucket(rel):
    half = REL_BUCKETS // 2
    max_exact = half // 2
    ret = jnp.where(rel > 0, half, 0)
    n = jnp.abs(rel)
    nf = jnp.maximum(n, 1).astype(F32)
    large = max_exact + (jnp.log(nf / max_exact) / math.log(REL_MAX_DIST / max_exact)
                         * (half - max_exact)).astype(jnp.int32)
    large = jnp.minimum(large, half - 1)
    return ret + jnp.where(n < max_exact, n, large)


def _toeplitz(vals, first):
    period = vals.shape[1]
    rolled = jnp.roll(vals, -first, axis=1)
    skew = jnp.tile(rolled, (1, TQ))[:, :TQ * (period - 1)].reshape(-1, TQ, period - 1)
    return skew[:, :, :TK]


def _bias_tiles(rel_bias):
    rel = jnp.arange(-2 * TQ + 1, TK + 1, dtype=jnp.int32)
    by_rel = rel_bias.astype(F32)[_t5_bucket(rel)].T
    key = jnp.arange(TK)[:, None]
    query = jnp.arange(TQ)[None, :]
    allowed = key < (query // CHUNK + 1) * CHUNK
    diag = jnp.where(allowed, jnp.swapaxes(_toeplitz(by_rel, 2 * TQ - 1), 1, 2), MASKED)
    near = jnp.swapaxes(_toeplitz(by_rel, 2 * TQ - 1 - TK), 1, 2)
    far = jnp.broadcast_to(by_rel[:, :1, None], near.shape)
    dead = jnp.full(near.shape, MASKED, F32)
    return jnp.stack([diag, near, far, dead], axis=1)


def _merge_kernel(x_ref, ysb_ref, ydf_ref, gpre_ref, wg_ref, wsb_ref, wdf_ref, wo_ref,
                  gpost_ref, o_ref):
    d = x_ref.shape[1]
    blocks = _row_blocks(x_ref.shape[0])
    gpre = gpre_ref[...]
    h = _rms(x_ref[blocks[0], :], gpre).astype(BF16)
    for k, rows in enumerate(blocks):
        h_now = h
        if k + 1 < len(blocks):
            h = _rms(x_ref[blocks[k + 1], :], gpre).astype(BF16)
        gates = jax.nn.sigmoid(_dot(h_now, wg_ref[...]))
        merged = (gates[:, :d] * _dot(ysb_ref[rows, :], wsb_ref[...])
                  + gates[:, d:] * _dot(ydf_ref[rows, :], wdf_ref[...]))
        o = _dot(merged.astype(BF16), wo_ref[...])
        o_ref[rows, :] = x_ref[rows, :] + _rms(o, gpost_ref[...])


def _merge(x2, ysb, ydf, gpre, wg, wsb, wdf, wo, gpost, tm):
    m, d = x2.shape

    def full(a):
        return pl.BlockSpec(a.shape, lambda i: (0, 0))

    def rows(a):
        return pl.BlockSpec((tm, a.shape[1]), lambda i: (i, 0))

    return pl.pallas_call(
        _merge_kernel,
        grid=(m // tm,),
        in_specs=[rows(x2), rows(ysb), rows(ydf), full(gpre), full(wg), full(wsb), full(wdf),
                  full(wo), full(gpost)],
        out_specs=pl.BlockSpec((tm, d), lambda i: (i, 0)),
        out_shape=jax.ShapeDtypeStruct((m, d), F32),
        compiler_params=pltpu.CompilerParams(
            dimension_semantics=("parallel",),
            vmem_limit_bytes=VMEM_LIMIT),
        name="merge",
    )(x2, ysb, ydf, gpre, wg, wsb, wdf, wo, gpost)


def _mlp_kernel(x_ref, gpre_ref, wup_ref, wdn_ref, gpost_ref, o_ref):
    blocks = _row_blocks(x_ref.shape[0])
    gpre = gpre_ref[...]
    h = _rms(x_ref[blocks[0], :], gpre).astype(BF16)
    for k, rows in enumerate(blocks):
        h_now = h
        if k + 1 < len(blocks):
            h = _rms(x_ref[blocks[k + 1], :], gpre).astype(BF16)
        acc = None
        for c in range(0, wup_ref.shape[1], COL_BLOCK):
            u = jnp.square(jnp.maximum(_dot(h_now, wup_ref[:, c:c + COL_BLOCK]), 0.0))
            part = _dot(u.astype(BF16), wdn_ref[c:c + COL_BLOCK, :])
            acc = part if acc is None else acc + part
        o_ref[rows, :] = x_ref[rows, :] + _rms(acc, gpost_ref[...])


def _mlp(x2, gpre, wup, wdn, gpost, tm):
    m, d = x2.shape

    def full(a):
        return pl.BlockSpec(a.shape, lambda i: (0, 0))

    return pl.pallas_call(
        _mlp_kernel,
        grid=(m // tm,),
        in_specs=[pl.BlockSpec((tm, d), lambda i: (i, 0)), full(gpre), full(wup), full(wdn),
                  full(gpost)],
        out_specs=pl.BlockSpec((tm, d), lambda i: (i, 0)),
        out_shape=jax.ShapeDtypeStruct((m, d), F32),
        compiler_params=pltpu.CompilerParams(
            dimension_semantics=("parallel",),
            vmem_limit_bytes=VMEM_LIMIT),
        name="mlp",
    )(x2, gpre, wup, wdn, gpost)


def kernel(x, w_in, w_sb_out, w_diff_out, w_o, lambda_q1, lambda_k1, lambda_q2, lambda_k2, w_subln, rel_bias, g_pre_mix, g_post_mix, g_pre_mlp, g_post_mlp, w_up, w_down):
    b, s, d = x.shape
    depth = w_in.shape[0]
    sb_width = SB_HEADS * HEAD_DIM
    diff_width = DIFF_HEADS * LANES
    n_qkv = 3 * sb_width + 3 * diff_width
    scale = HEAD_DIM ** -0.5
    col_scale = jnp.ones((n_qkv,), F32)
    col_scale = col_scale.at[:sb_width].set(scale)
    col_scale = col_scale.at[3 * sb_width:3 * sb_width + diff_width].set(scale)
    bias_tiles = _bias_tiles(rel_bias)

    x2 = x.reshape(b * s, d)
    for l in range(depth):
        lam_init = 0.8 - 0.6 * math.exp(-0.3 * l)
        w_qkv = (w_in[l][:, :n_qkv] * col_scale).astype(BF16)
        w_gate = w_in[l][:, n_qkv:].astype(BF16)
        d_q0 = 3 * sb_width
        d_v0 = d_q0 + 2 * diff_width
        qkv, q_t, v_t = _proj(x2, g_pre_mix[l][None], w_qkv, 512, b, s,
                              (d_q0, d_q0 + diff_width), (d_v0, d_v0 + diff_width))
        qkv = qkv.reshape(b, s, n_qkv)
        y_sb = _sb_attention(qkv, b, s)
        y_diff = _diff_attention(q_t, qkv, v_t, bias_tiles, lambda_q1[l][None],
                                 lambda_k1[l][None], lambda_q2[l][None], lambda_k2[l][None],
                                 w_subln[l][:, None], b, s, lam_init)
        x2 = _merge(x2, y_sb.reshape(b * s, sb_width), y_diff.reshape(b * s, diff_width),
                    g_pre_mix[l][None], w_gate, w_sb_out[l].astype(BF16),
                    w_diff_out[l].astype(BF16), w_o[l].astype(BF16), g_post_mix[l][None], 512)
        x2 = _mlp(x2, g_pre_mlp[l][None], w_up[l].astype(BF16), w_down[l].astype(BF16),
                  g_post_mlp[l][None], 512)
    return x2.reshape(b, s, d)
```

```python
import functools
import math

import jax
import jax.numpy as jnp
from jax import lax
from jax.experimental import pallas as pl
from jax.experimental.pallas import tpu as pltpu

F32 = jnp.float32
BF16 = jnp.bfloat16

NORM_EPS = 1e-6
CHUNK = 64
SB_HEADS = 8
HEAD_DIM = 64
DIFF_HEADS = 4
LANES = 128
REL_BUCKETS = 32
REL_MAX_DIST = 128
MASKED = -1e30
LOG2E = 1.4426950408889634
DEAD_LOG = -111.0

VMEM_LIMIT = 56 * 1024 * 1024

TQ = 256
TK = 256
GROUPS = 2
DIFF_GROUPS = 4


def _rms(xf, g):
    ms = jnp.mean(xf * xf, axis=-1, keepdims=True)
    return xf * lax.rsqrt(ms + NORM_EPS) * g


def _dot(a, b):
    return jnp.dot(a, b, preferred_element_type=F32)


def _dot_nt(a, b):
    return lax.dot_general(a, b, (((1,), (1,)), ((), ())), preferred_element_type=F32)


def _half_lane_queries(q_ref):
    lane = lax.broadcasted_iota(jnp.int32, (TQ, LANES), 1)
    qs = []
    for g in range(GROUPS):
        q_all = q_ref[0, :, g * LANES:(g + 1) * LANES]
        for half in range(2):
            keep = (lane >= half * HEAD_DIM) & (lane < (half + 1) * HEAD_DIM)
            qs.append(jnp.where(keep, q_all, jnp.zeros_like(q_all)))
    return qs


def _key_tile(ref, kb, g):
    start = pl.multiple_of(kb * TK, TK)
    return ref[0, pl.ds(start, TK), g * LANES:(g + 1) * LANES]


ROW_BLOCK = 256
COL_BLOCK = 1024


def _row_blocks(tm):
    return [slice(r, r + ROW_BLOCK) for r in range(0, tm, ROW_BLOCK)]


def _proj_kernel(x_ref, g_ref, w_ref, o_ref, qt_ref, vt_ref, *, q_cols, v_cols):
    blocks = _row_blocks(x_ref.shape[0])
    g = g_ref[...]
    h = _rms(x_ref[blocks[0], :], g).astype(BF16)
    for k, rows in enumerate(blocks):
        h_now = h
        if k + 1 < len(blocks):
            h = _rms(x_ref[blocks[k + 1], :], g).astype(BF16)
        for c in range(0, w_ref.shape[1], COL_BLOCK):
            res = _dot(h_now, w_ref[:, c:c + COL_BLOCK])
            o_ref[rows, c:c + COL_BLOCK] = res.astype(BF16)
            if c <= q_cols[0] and q_cols[1] <= c + COL_BLOCK:
                qt_ref[0, k] = res[:, q_cols[0] - c:q_cols[1] - c].T.astype(BF16)
            if c <= v_cols[0] and v_cols[1] <= c + COL_BLOCK:
                v_t = res[:, v_cols[0] - c:v_cols[1] - c].T.astype(BF16)
                for head in range(DIFF_HEADS):
                    top = head * VT_ROWS
                    vt_ref[0, k, top:top + LANES, :] = v_t[head * LANES:(head + 1) * LANES, :]
                    vt_ref[0, k, top + LANES:top + VT_ROWS, :] = jnp.ones((ONES_ROWS, TK), BF16)


def _proj(x2, g, w, tm, b, s, q_cols, v_cols):
    m, d = x2.shape
    n = w.shape[1]
    per_batch = s // tm
    tiles = tm // ROW_BLOCK
    return pl.pallas_call(
        functools.partial(_proj_kernel, q_cols=q_cols, v_cols=v_cols),
        grid=(m // tm,),
        in_specs=[
            pl.BlockSpec((tm, d), lambda i: (i, 0)),
            pl.BlockSpec((1, d), lambda i: (0, 0)),
            pl.BlockSpec((d, n), lambda i: (0, 0)),
        ],
        out_specs=[
            pl.BlockSpec((tm, n), lambda i: (i, 0)),
            pl.BlockSpec((1, tiles, q_cols[1] - q_cols[0], TQ),
                         lambda i: (i // per_batch, i % per_batch, 0, 0)),
            pl.BlockSpec((1, tiles, DIFF_HEADS * VT_ROWS, TK),
                         lambda i: (i // per_batch, i % per_batch, 0, 0)),
        ],
        out_shape=[
            jax.ShapeDtypeStruct((m, n), BF16),
            jax.ShapeDtypeStruct((b, s // TQ, q_cols[1] - q_cols[0], TQ), BF16),
            jax.ShapeDtypeStruct((b, s // TK, DIFF_HEADS * VT_ROWS, TK), BF16),
        ],
        compiler_params=pltpu.CompilerParams(
            dimension_semantics=("parallel",),
            vmem_limit_bytes=VMEM_LIMIT),
        name="proj",
    )(x2, g, w)


def _sb_scores(qs, ks):
    return [_dot_nt(q, k) for q, k in zip(qs, ks)]


def _sb_weights(zs, vs, upper, states, mask):
    n = len(zs)
    log_betas, log_oms, his, los = [], [], [], []
    for z in zs:
        soft = jnp.log(1.0 + jnp.exp2(jnp.abs(z) * -LOG2E))
        log_beta = jnp.minimum(z, 0.0) - soft
        log_om = log_beta - z
        if mask is not None:
            log_om = jnp.where(mask, log_om, 0.0)
        hi = log_om.astype(BF16)
        log_betas.append(log_beta)
        log_oms.append(log_om)
        his.append(hi)
        los.append((log_om - hi.astype(F32)).astype(BF16))
    tails = [_dot(his[i], upper) + _dot(los[i], upper) for i in range(n)]
    ws = []
    for i in range(n):
        w = jnp.exp(log_betas[i] + tails[i])
        if mask is not None:
            w = jnp.where(mask, w, 0.0)
        ws.append(w.astype(BF16))
    new = []
    for i in range(n):
        c, acc = states[i]
        acc = acc + jnp.exp(c) * _dot(ws[i], vs[i])
        c = c + jnp.sum(log_oms[i], axis=1, keepdims=True)
        new.append((c, acc))
    return tuple(new)


def _sb_kernel(q_ref, k_ref, v_ref, o_ref):
    qi = pl.program_id(2)
    row = lax.broadcasted_iota(jnp.int32, (TQ, TK), 0)
    col = lax.broadcasted_iota(jnp.int32, (TQ, TK), 1)
    causal = col < row
    upper = jnp.where(row > col, 1.0, 0.0).astype(BF16)
    qs = _half_lane_queries(q_ref)
    n_chains = 2 * GROUPS

    def scores(kb):
        return _sb_scores(qs, [_key_tile(k_ref, kb, i // 2) for i in range(n_chains)])

    def weights(kb, zs, states, mask):
        vs = [_key_tile(v_ref, kb, i // 2) for i in range(n_chains)]
        return _sb_weights(zs, vs, upper, states, mask)

    def any_weight_left(states):
        c_max = functools.reduce(jnp.maximum, [jnp.max(st[0]) for st in states])
        return c_max > DEAD_LOG

    def first_two(states):
        z_diag, z_next = scores(qi), scores(qi - 1)
        states = weights(qi, z_diag, states, causal)
        return weights(qi - 1, z_next, states, None)

    def step(carry):
        j, _, states = carry
        states = weights(qi - 1 - j, scores(qi - 1 - j), states, None)
        return j + 1, any_weight_left(states), states

    init = tuple((jnp.zeros((TQ, 1), F32), jnp.zeros((TQ, LANES), F32))
                 for _ in range(n_chains))
    states = lax.cond(qi >= 1, first_two,
                      lambda st: weights(qi, scores(qi), st, causal), init)
    _, _, states = lax.while_loop(lambda carry: (carry[0] < qi) & carry[1], step,
                                  (jnp.int32(1), any_weight_left(states), states))

    lane = lax.broadcasted_iota(jnp.int32, (TQ, LANES), 1)
    for g in range(GROUPS):
        out = jnp.where(lane < HEAD_DIM, states[2 * g][1], states[2 * g + 1][1])
        o_ref[0, :, g * LANES:(g + 1) * LANES] = out.astype(BF16)


def _sb_attention(qkv, b, s):
    n_blk = SB_HEADS * HEAD_DIM // (GROUPS * LANES)
    width = GROUPS * LANES
    return pl.pallas_call(
        _sb_kernel,
        grid=(b, n_blk, s // TQ),
        in_specs=[
            pl.BlockSpec((1, TQ, width), lambda bi, p, qi: (bi, qi, p)),
            pl.BlockSpec((1, s, width), lambda bi, p, qi: (bi, 0, n_blk + p)),
            pl.BlockSpec((1, s, width), lambda bi, p, qi: (bi, 0, 2 * n_blk + p)),
        ],
        out_specs=pl.BlockSpec((1, TQ, width), lambda bi, p, qi: (bi, qi, p)),
        out_shape=jax.ShapeDtypeStruct((b, s, SB_HEADS * HEAD_DIM), BF16),
        compiler_params=pltpu.CompilerParams(
            dimension_semantics=("parallel", "parallel", "arbitrary"),
            vmem_limit_bytes=VMEM_LIMIT),
        name="sb_attention",
    )(qkv, qkv, qkv)


BIAS_DIAG, BIAS_NEAR, BIAS_FAR, BIAS_DEAD = 0, 1, 2, 3
ONES_ROWS = 16
VT_ROWS = LANES + ONES_ROWS


def _diff_kernel(qt_ref, k_ref, vt_ref, bias_ref, lq1_ref, lk1_ref, lq2_ref, lk2_ref, g_ref,
                 o_ref, z_a, z_b, p_a, p_b, m_ref, alpha_a, alpha_b, acc_ref, *, lam_init):
    qi = pl.program_id(2)
    n_chains = 2 * DIFF_GROUPS
    feature = lax.broadcasted_iota(jnp.int32, (LANES, TQ), 0)
    qts = []
    for g in range(DIFF_GROUPS):
        qt_all = qt_ref[0, 0, g * LANES:(g + 1) * LANES, :]
        for half in range(2):
            keep = (feature >= half * HEAD_DIM) & (feature < (half + 1) * HEAD_DIM)
            qts.append(jnp.where(keep, qt_all, jnp.zeros_like(qt_all)))

    def key_block(j):
        return jnp.clip(qi - j, 0, qi)

    def bias_kind(j):
        return jnp.where(j > qi, BIAS_DEAD, jnp.minimum(j, BIAS_FAR))

    def scores(j, z_ref, chains=range(2 * DIFF_GROUPS)):
        kb = key_block(j)
        for i in chains:
            z_ref[i] = _dot(_key_tile(k_ref, kb, i // 2), qts[i])

    def normalise(j, z_ref, p_ref, alpha_ref, per_element_bias, chains=range(2 * DIFF_GROUPS)):
        kind = bias_kind(j)
        for i in chains:
            m_old = m_ref[i]
            if per_element_bias:
                bias = bias_ref.at[i // 2, kind]
                m_new = jnp.maximum(m_old, jnp.max(z_ref[i] + bias[...], axis=0, keepdims=True))
                m_ref[i] = m_new
                p = jnp.exp2(z_ref[i] + bias[...] - m_new)
            else:
                c = bias_ref[i // 2, kind, 0:1, 0:1]
                m_new = jnp.maximum(m_old, jnp.max(z_ref[i], axis=0, keepdims=True) + c)
                m_ref[i] = m_new
                p = jnp.exp2(z_ref[i] - (m_new - c))
            alpha_ref[i] = jnp.exp2(m_old - m_new)
            p_ref[i] = p.astype(BF16)

    def accumulate(j, p_ref, alpha_ref, chains=range(2 * DIFF_GROUPS)):
        kb = key_block(j)
        for i in chains:
            g = i // 2
            vt = vt_ref[0, kb, g * VT_ROWS:(g + 1) * VT_ROWS, :]
            acc_ref[i] = alpha_ref[i] * acc_ref[i] + _dot(vt, p_ref[i])

    m_ref[...] = jnp.full(m_ref.shape, MASKED, F32)
    acc_ref[...] = jnp.zeros(acc_ref.shape, F32)
    scores(0, z_a)
    normalise(0, z_a, p_a, alpha_a, True)
    scores(1, z_b)
    accumulate(0, p_a, alpha_a)
    normalise(1, z_b, p_b, alpha_b, True)
    scores(2, z_a)

    def trip(t, carry):
        j = 2 * t
        for i in range(n_chains):
            normalise(j, z_a, p_a, alpha_a, False, (i,))
            accumulate(j - 1, p_b, alpha_b, (i,))
            scores(j + 1, z_b, (i,))
        for i in range(n_chains):
            normalise(j + 1, z_b, p_b, alpha_b, False, (i,))
            accumulate(j, p_a, alpha_a, (i,))
            scores(j + 2, z_a, (i,))
        return carry

    n_trips = (qi + 2) // 2
    lax.fori_loop(1, n_trips, trip, 0)
    accumulate(2 * n_trips - 1, p_b, alpha_b)

    lam = (jnp.exp(jnp.sum(lq1_ref[...] * lk1_ref[...], axis=1, keepdims=True))
           - jnp.exp(jnp.sum(lq2_ref[...] * lk2_ref[...], axis=1, keepdims=True))
           + lam_init)
    for g in range(DIFF_GROUPS):
        normed = []
        for i in (2 * g, 2 * g + 1):
            normed.append(acc_ref[i, :LANES, :] * (1.0 / acc_ref[i, LANES:LANES + 1, :]))
        yt = normed[0] - lam * normed[1]
        ms = jnp.mean(yt * yt, axis=0, keepdims=True)
        yt = yt * lax.rsqrt(ms + NORM_EPS) * g_ref[...] * (1.0 - lam_init)
        o_ref[0, :, g * LANES:(g + 1) * LANES] = yt.T.astype(BF16)


def _diff_attention(qt, qkv, vt, bias_tiles, lq1, lk1, lq2, lk2, g_subln, b, s, lam_init):
    n_blk = DIFF_HEADS // DIFF_GROUPS
    first = 3 * SB_HEADS * HEAD_DIM // (DIFF_GROUPS * LANES)
    width = DIFF_GROUPS * LANES
    n_chains = 2 * DIFF_GROUPS
    vec = pl.BlockSpec((1, HEAD_DIM), lambda bi, h, qi: (0, 0))
    return pl.pallas_call(
        functools.partial(_diff_kernel, lam_init=lam_init),
        grid=(b, n_blk, s // TQ),
        in_specs=[
            pl.BlockSpec((1, 1, width, TQ), lambda bi, h, qi: (bi, qi, h, 0)),
            pl.BlockSpec((1, s, width), lambda bi, h, qi: (bi, 0, first + n_blk + h)),
            pl.BlockSpec((1, s // TK, DIFF_GROUPS * VT_ROWS, TK), lambda bi, h, qi: (bi, 0, h, 0)),
            pl.BlockSpec((DIFF_GROUPS, 4, TK, TQ), lambda bi, h, qi: (h, 0, 0, 0)),
            vec, vec, vec, vec,
            pl.BlockSpec((LANES, 1), lambda bi, h, qi: (0, 0)),
        ],
        out_specs=pl.BlockSpec((1, TQ, width), lambda bi, h, qi: (bi, qi, h)),
        out_shape=jax.ShapeDtypeStruct((b, s, DIFF_HEADS * LANES), BF16),
        scratch_shapes=[
            pltpu.VMEM((n_chains, TK, TQ), F32), pltpu.VMEM((n_chains, TK, TQ), F32),
            pltpu.VMEM((n_chains, TK, TQ), BF16), pltpu.VMEM((n_chains, TK, TQ), BF16),
            pltpu.VMEM((n_chains, 1, TQ), F32),
            pltpu.VMEM((n_chains, 1, TQ), F32), pltpu.VMEM((n_chains, 1, TQ), F32),
            pltpu.VMEM((n_chains, VT_ROWS, TQ), F32),
        ],
        compiler_params=pltpu.CompilerParams(
            dimension_semantics=("parallel", "parallel", "arbitrary"),
            vmem_limit_bytes=VMEM_LIMIT),
        name="diff_attention",
    )(qt, qkv, vt, bias_tiles, lq1, lk1, lq2, lk2, g_subln)


def _t5_bucket(rel):
    half = REL_BUCKETS // 2
    max_exact = half // 2
    ret = jnp.where(rel > 0, half, 0)
    n = jnp.abs(rel)
    nf = jnp.maximum(n, 1).astype(F32)
    large = max_exact + (jnp.log(nf / max_exact) / math.log(REL_MAX_DIST / max_exact)
                         * (half - max_exact)).astype(jnp.int32)
    large = jnp.minimum(large, half - 1)
    return ret + jnp.where(n < max_exact, n, large)


def _toeplitz(vals, first):
    period = vals.shape[1]
    rolled = jnp.roll(vals, -first, axis=1)
    skew = jnp.tile(rolled, (1, TQ))[:, :TQ * (period - 1)].reshape(-1, TQ, period - 1)
    return skew[:, :, :TK]


def _bias_tiles(rel_bias):
    rel = jnp.arange(-2 * TQ + 1, TK + 1, dtype=jnp.int32)
    by_rel = rel_bias.astype(F32)[_t5_bucket(rel)].T * LOG2E
    key = jnp.arange(TK)[:, None]
    query = jnp.arange(TQ)[None, :]
    allowed = key < (query // CHUNK + 1) * CHUNK
    diag = jnp.where(allowed, jnp.swapaxes(_toeplitz(by_rel, 2 * TQ - 1), 1, 2), MASKED)
    near = jnp.swapaxes(_toeplitz(by_rel, 2 * TQ - 1 - TK), 1, 2)
    far = jnp.broadcast_to(by_rel[:, :1, None], near.shape)
    dead = jnp.full(near.shape, MASKED, F32)
    return jnp.stack([diag, near, far, dead], axis=1)


def _merge_kernel(x_ref, ysb_ref, ydf_ref, gpre_ref, wg_ref, wsb_ref, wdf_ref, wo_ref,
                  gpost_ref, o_ref):
    d = x_ref.shape[1]
    blocks = _row_blocks(x_ref.shape[0])
    gpre = gpre_ref[...]
    h = _rms(x_ref[blocks[0], :], gpre).astype(BF16)
    for k, rows in enumerate(blocks):
        h_now = h
        if k + 1 < len(blocks):
            h = _rms(x_ref[blocks[k + 1], :], gpre).astype(BF16)
        gates = jax.nn.sigmoid(_dot(h_now, wg_ref[...]))
        merged = (gates[:, :d] * _dot(ysb_ref[rows, :], wsb_ref[...])
                  + gates[:, d:] * _dot(ydf_ref[rows, :], wdf_ref[...]))
        o = _dot(merged.astype(BF16), wo_ref[...])
        o_ref[rows, :] = x_ref[rows, :] + _rms(o, gpost_ref[...])


def _merge(x2, ysb, ydf, gpre, wg, wsb, wdf, wo, gpost, tm):
    m, d = x2.shape

    def full(a):
        return pl.BlockSpec(a.shape, lambda i: (0, 0))

    def rows(a):
        return pl.BlockSpec((tm, a.shape[1]), lambda i: (i, 0))

    return pl.pallas_call(
        _merge_kernel,
        grid=(m // tm,),
        in_specs=[rows(x2), rows(ysb), rows(ydf), full(gpre), full(wg), full(wsb), full(wdf),
                  full(wo), full(gpost)],
        out_specs=pl.BlockSpec((tm, d), lambda i: (i, 0)),
        out_shape=jax.ShapeDtypeStruct((m, d), F32),
        compiler_params=pltpu.CompilerParams(
            dimension_semantics=("parallel",),
            vmem_limit_bytes=VMEM_LIMIT),
        name="merge",
    )(x2, ysb, ydf, gpre, wg, wsb, wdf, wo, gpost)


def _mlp_kernel(x_ref, gpre_ref, wup_ref, wdn_ref, gpost_ref, o_ref):
    blocks = _row_blocks(x_ref.shape[0])
    gpre = gpre_ref[...]
    h = _rms(x_ref[blocks[0], :], gpre).astype(BF16)
    for k, rows in enumerate(blocks):
        h_now = h
        if k + 1 < len(blocks):
            h = _rms(x_ref[blocks[k + 1], :], gpre).astype(BF16)
        acc = None
        for c in range(0, wup_ref.shape[1], COL_BLOCK):
            u = jnp.square(jnp.maximum(_dot(h_now, wup_ref[:, c:c + COL_BLOCK]), 0.0))
            part = _dot(u.astype(BF16), wdn_ref[c:c + COL_BLOCK, :])
            acc = part if acc is None else acc + part
        o_ref[rows, :] = x_ref[rows, :] + _rms(acc, gpost_ref[...])


def _mlp(x2, gpre, wup, wdn, gpost, tm):
    m, d = x2.shape

    def full(a):
        return pl.BlockSpec(a.shape, lambda i: (0, 0))

    return pl.pallas_call(
        _mlp_kernel,
        grid=(m // tm,),
        in_specs=[pl.BlockSpec((tm, d), lambda i: (i, 0)), full(gpre), full(wup), full(wdn),
                  full(gpost)],
        out_specs=pl.BlockSpec((tm, d), lambda i: (i, 0)),
        out_shape=jax.ShapeDtypeStruct((m, d), F32),
        compiler_params=pltpu.CompilerParams(
            dimension_semantics=("parallel",),
            vmem_limit_bytes=VMEM_LIMIT),
        name="mlp",
    )(x2, gpre, wup, wdn, gpost)


def kernel(x, w_in, w_sb_out, w_diff_out, w_o, lambda_q1, lambda_k1, lambda_q2, lambda_k2, w_subln, rel_bias, g_pre_mix, g_post_mix, g_pre_mlp, g_post_mlp, w_up, w_down):
    b, s, d = x.shape
    depth = w_in.shape[0]
    sb_width = SB_HEADS * HEAD_DIM
    diff_width = DIFF_HEADS * LANES
    n_qkv = 3 * sb_width + 3 * diff_width
    scale = HEAD_DIM ** -0.5
    col_scale = jnp.ones((n_qkv,), F32)
    col_scale = col_scale.at[:sb_width].set(scale)
    col_scale = col_scale.at[3 * sb_width:3 * sb_width + diff_width].set(scale * LOG2E)
    bias_tiles = _bias_tiles(rel_bias)

    x2 = x.reshape(b * s, d)
    for l in range(depth):
        lam_init = 0.8 - 0.6 * math.exp(-0.3 * l)
        w_qkv = (w_in[l][:, :n_qkv] * col_scale).astype(BF16)
        w_gate = w_in[l][:, n_qkv:].astype(BF16)
        d_q0 = 3 * sb_width
        d_v0 = d_q0 + 2 * diff_width
        qkv, q_t, v_t = _proj(x2, g_pre_mix[l][None], w_qkv, 1024, b, s,
                              (d_q0, d_q0 + diff_width), (d_v0, d_v0 + diff_width))
        qkv = qkv.reshape(b, s, n_qkv)
        y_sb = _sb_attention(qkv, b, s)
        y_diff = _diff_attention(q_t, qkv, v_t, bias_tiles, lambda_q1[l][None],
                                 lambda_k1[l][None], lambda_q2[l][None], lambda_k2[l][None],
                                 w_subln[l][:, None], b, s, lam_init)
        x2 = _merge(x2, y_sb.reshape(b * s, sb_width), y_diff.reshape(b * s, diff_width),
                    g_pre_mix[l][None], w_gate, w_sb_out[l].astype(BF16),
                    w_diff_out[l].astype(BF16), w_o[l].astype(BF16), g_post_mix[l][None], 1024)
        x2 = _mlp(x2, g_pre_mlp[l][None], w_up[l].astype(BF16), w_down[l].astype(BF16),
                  g_post_mlp[l][None], 1024)
    return x2.reshape(b, s, d)
```

```python
import functools
import math

import jax
import jax.numpy as jnp
from jax import lax
from jax.experimental import pallas as pl
from jax.experimental.pallas import tpu as pltpu

F32 = jnp.float32
BF16 = jnp.bfloat16

NORM_EPS = 1e-6
CHUNK = 64
SB_HEADS = 8
HEAD_DIM = 64
DIFF_HEADS = 4
LANES = 128
REL_BUCKETS = 32
REL_MAX_DIST = 128
MASKED = -1e30
LOG2E = 1.4426950408889634
DEAD_LOG = -111.0

VMEM_LIMIT = 56 * 1024 * 1024

TQ = 256
TK = 256
GROUPS = 4
DIFF_GROUPS = 4


def _rms(xf, g):
    ms = jnp.mean(xf * xf, axis=-1, keepdims=True)
    return xf * lax.rsqrt(ms + NORM_EPS) * g


def _dot(a, b):
    return jnp.dot(a, b, preferred_element_type=F32)


def _dot_nt(a, b):
    return lax.dot_general(a, b, (((1,), (1,)), ((), ())), preferred_element_type=F32)


def _half_lane_queries(q_ref):
    lane = lax.broadcasted_iota(jnp.int32, (TQ, LANES), 1)
    qs = []
    for g in range(GROUPS):
        q_all = q_ref[0, :, g * LANES:(g + 1) * LANES]
        for half in range(2):
            keep = (lane >= half * HEAD_DIM) & (lane < (half + 1) * HEAD_DIM)
            qs.append(jnp.where(keep, q_all, jnp.zeros_like(q_all)))
    return qs


def _key_tile(ref, kb, g):
    start = pl.multiple_of(kb * TK, TK)
    return ref[0, pl.ds(start, TK), g * LANES:(g + 1) * LANES]


ROW_BLOCK = 256
COL_BLOCK = 1024


def _row_blocks(tm):
    return [slice(r, r + ROW_BLOCK) for r in range(0, tm, ROW_BLOCK)]


def _proj_kernel(x_ref, g_ref, w_ref, o_ref, qt_ref, vt_ref, *, q_cols, v_cols):
    blocks = _row_blocks(x_ref.shape[0])
    g = g_ref[...]
    h = _rms(x_ref[blocks[0], :], g).astype(BF16)
    for k, rows in enumerate(blocks):
        h_now = h
        if k + 1 < len(blocks):
            h = _rms(x_ref[blocks[k + 1], :], g).astype(BF16)
        for c in range(0, w_ref.shape[1], COL_BLOCK):
            res = _dot(h_now, w_ref[:, c:c + COL_BLOCK])
            o_ref[rows, c:c + COL_BLOCK] = res.astype(BF16)
            if c <= q_cols[0] and q_cols[1] <= c + COL_BLOCK:
                qt_ref[0, k] = res[:, q_cols[0] - c:q_cols[1] - c].T.astype(BF16)
            if c <= v_cols[0] and v_cols[1] <= c + COL_BLOCK:
                v_t = res[:, v_cols[0] - c:v_cols[1] - c].T.astype(BF16)
                for head in range(DIFF_HEADS):
                    top = head * VT_ROWS
                    vt_ref[0, k, top:top + LANES, :] = v_t[head * LANES:(head + 1) * LANES, :]
                    vt_ref[0, k, top + LANES:top + VT_ROWS, :] = jnp.ones((ONES_ROWS, TK), BF16)


def _proj(x2, g, w, tm, b, s, q_cols, v_cols):
    m, d = x2.shape
    n = w.shape[1]
    per_batch = s // tm
    tiles = tm // ROW_BLOCK
    return pl.pallas_call(
        functools.partial(_proj_kernel, q_cols=q_cols, v_cols=v_cols),
        grid=(m // tm,),
        in_specs=[
            pl.BlockSpec((tm, d), lambda i: (i, 0)),
            pl.BlockSpec((1, d), lambda i: (0, 0)),
            pl.BlockSpec((d, n), lambda i: (0, 0)),
        ],
        out_specs=[
            pl.BlockSpec((tm, n), lambda i: (i, 0)),
            pl.BlockSpec((1, tiles, q_cols[1] - q_cols[0], TQ),
                         lambda i: (i // per_batch, i % per_batch, 0, 0)),
            pl.BlockSpec((1, tiles, DIFF_HEADS * VT_ROWS, TK),
                         lambda i: (i // per_batch, i % per_batch, 0, 0)),
        ],
        out_shape=[
            jax.ShapeDtypeStruct((m, n), BF16),
            jax.ShapeDtypeStruct((b, s // TQ, q_cols[1] - q_cols[0], TQ), BF16),
            jax.ShapeDtypeStruct((b, s // TK, DIFF_HEADS * VT_ROWS, TK), BF16),
        ],
        compiler_params=pltpu.CompilerParams(
            dimension_semantics=("parallel",),
            vmem_limit_bytes=VMEM_LIMIT),
        name="proj",
    )(x2, g, w)


def _sb_scores(qs, ks):
    return [_dot_nt(q, k) for q, k in zip(qs, ks)]


def _sb_weights(zs, vs, upper, states, mask):
    n = len(zs)
    log_betas, log_oms, his, los = [], [], [], []
    for z in zs:
        soft = jnp.log(1.0 + jnp.exp2(jnp.abs(z) * -LOG2E))
        log_beta = jnp.minimum(z, 0.0) - soft
        log_om = log_beta - z
        if mask is not None:
            log_om = jnp.where(mask, log_om, 0.0)
        hi = log_om.astype(BF16)
        log_betas.append(log_beta)
        log_oms.append(log_om)
        his.append(hi)
        los.append((log_om - hi.astype(F32)).astype(BF16))
    tails = [_dot(his[i], upper) + _dot(los[i], upper) for i in range(n)]
    ws = []
    for i in range(n):
        w = jnp.exp(log_betas[i] + tails[i])
        if mask is not None:
            w = jnp.where(mask, w, 0.0)
        ws.append(w.astype(BF16))
    new = []
    for i in range(n):
        c, acc = states[i]
        acc = acc + jnp.exp(c) * _dot(ws[i], vs[i])
        c = c + jnp.sum(log_oms[i], axis=1, keepdims=True)
        new.append((c, acc))
    return tuple(new)


def _sb_kernel(q_ref, k_ref, v_ref, o_ref):
    qi = pl.program_id(2)
    row = lax.broadcasted_iota(jnp.int32, (TQ, TK), 0)
    col = lax.broadcasted_iota(jnp.int32, (TQ, TK), 1)
    causal = col < row
    upper = jnp.where(row > col, 1.0, 0.0).astype(BF16)
    qs = _half_lane_queries(q_ref)
    n_chains = 2 * GROUPS

    def scores(kb):
        return _sb_scores(qs, [_key_tile(k_ref, kb, i // 2) for i in range(n_chains)])

    def weights(kb, zs, states, mask):
        vs = [_key_tile(v_ref, kb, i // 2) for i in range(n_chains)]
        return _sb_weights(zs, vs, upper, states, mask)

    def any_weight_left(states):
        c_max = functools.reduce(jnp.maximum, [jnp.max(st[0]) for st in states])
        return c_max > DEAD_LOG

    def first_two(states):
        z_diag, z_next = scores(qi), scores(qi - 1)
        states = weights(qi, z_diag, states, causal)
        return weights(qi - 1, z_next, states, None)

    def step(carry):
        j, _, states = carry
        states = weights(qi - 1 - j, scores(qi - 1 - j), states, None)
        return j + 1, any_weight_left(states), states

    init = tuple((jnp.zeros((TQ, 1), F32), jnp.zeros((TQ, LANES), F32))
                 for _ in range(n_chains))
    states = lax.cond(qi >= 1, first_two,
                      lambda st: weights(qi, scores(qi), st, causal), init)
    _, _, states = lax.while_loop(lambda carry: (carry[0] < qi) & carry[1], step,
                                  (jnp.int32(1), any_weight_left(states), states))

    lane = lax.broadcasted_iota(jnp.int32, (TQ, LANES), 1)
    for g in range(GROUPS):
        out = jnp.where(lane < HEAD_DIM, states[2 * g][1], states[2 * g + 1][1])
        o_ref[0, :, g * LANES:(g + 1) * LANES] = out.astype(BF16)


def _sb_attention(qkv, b, s):
    n_blk = SB_HEADS * HEAD_DIM // (GROUPS * LANES)
    width = GROUPS * LANES
    return pl.pallas_call(
        _sb_kernel,
        grid=(b, n_blk, s // TQ),
        in_specs=[
            pl.BlockSpec((1, TQ, width), lambda bi, p, qi: (bi, qi, p)),
            pl.BlockSpec((1, s, width), lambda bi, p, qi: (bi, 0, n_blk + p)),
            pl.BlockSpec((1, s, width), lambda bi, p, qi: (bi, 0, 2 * n_blk + p)),
        ],
        out_specs=pl.BlockSpec((1, TQ, width), lambda bi, p, qi: (bi, qi, p)),
        out_shape=jax.ShapeDtypeStruct((b, s, SB_HEADS * HEAD_DIM), BF16),
        compiler_params=pltpu.CompilerParams(
            dimension_semantics=("parallel", "parallel", "arbitrary"),
            vmem_limit_bytes=VMEM_LIMIT),
        name="sb_attention",
    )(qkv, qkv, qkv)


BIAS_DIAG, BIAS_NEAR, BIAS_FAR, BIAS_DEAD = 0, 1, 2, 3
ONES_ROWS = 16
VT_ROWS = LANES + ONES_ROWS


def _diff_kernel(qt_ref, k_ref, vt_ref, bias_ref, lq1_ref, lk1_ref, lq2_ref, lk2_ref, g_ref,
                 o_ref, z_a, z_b, p_a, p_b, m_ref, alpha_a, alpha_b, acc_ref, *, lam_init):
    qi = pl.program_id(2)
    n_chains = 2 * DIFF_GROUPS
    feature = lax.broadcasted_iota(jnp.int32, (LANES, TQ), 0)
    qts = []
    for g in range(DIFF_GROUPS):
        qt_all = qt_ref[0, 0, g * LANES:(g + 1) * LANES, :]
        for half in range(2):
            keep = (feature >= half * HEAD_DIM) & (feature < (half + 1) * HEAD_DIM)
            qts.append(jnp.where(keep, qt_all, jnp.zeros_like(qt_all)))

    def key_block(j):
        return jnp.clip(qi - j, 0, qi)

    def bias_kind(j):
        return jnp.where(j > qi, BIAS_DEAD, jnp.minimum(j, BIAS_FAR))

    def scores(j, z_ref, chains=range(2 * DIFF_GROUPS)):
        kb = key_block(j)
        for i in chains:
            z_ref[i] = _dot(_key_tile(k_ref, kb, i // 2), qts[i])

    def normalise(j, z_ref, p_ref, alpha_ref, per_element_bias, chains=range(2 * DIFF_GROUPS)):
        kind = bias_kind(j)
        for i in chains:
            m_old = m_ref[i]
            if per_element_bias:
                bias = bias_ref.at[i // 2, kind]
                m_new = jnp.maximum(m_old, jnp.max(z_ref[i] + bias[...], axis=0, keepdims=True))
                m_ref[i] = m_new
                p = jnp.exp2(z_ref[i] + bias[...] - m_new)
            else:
                c = bias_ref[i // 2, kind, 0:1, 0:1]
                m_new = jnp.maximum(m_old, jnp.max(z_ref[i], axis=0, keepdims=True) + c)
                m_ref[i] = m_new
                p = jnp.exp2(z_ref[i] - (m_new - c))
            alpha_ref[i] = jnp.exp2(m_old - m_new)
            p_ref[i] = p.astype(BF16)

    def accumulate(j, p_ref, alpha_ref, chains=range(2 * DIFF_GROUPS)):
        kb = key_block(j)
        for i in chains:
            g = i // 2
            vt = vt_ref[0, kb, g * VT_ROWS:(g + 1) * VT_ROWS, :]
            acc_ref[i] = alpha_ref[i] * acc_ref[i] + _dot(vt, p_ref[i])

    m_ref[...] = jnp.full(m_ref.shape, MASKED, F32)
    acc_ref[...] = jnp.zeros(acc_ref.shape, F32)
    scores(0, z_a)
    normalise(0, z_a, p_a, alpha_a, True)
    scores(1, z_b)
    accumulate(0, p_a, alpha_a)
    normalise(1, z_b, p_b, alpha_b, True)
    scores(2, z_a)

    def trip(t, carry):
        j = 2 * t
        for i in range(n_chains):
            normalise(j, z_a, p_a, alpha_a, False, (i,))
            accumulate(j - 1, p_b, alpha_b, (i,))
            scores(j + 1, z_b, (i,))
        for i in range(n_chains):
            normalise(j + 1, z_b, p_b, alpha_b, False, (i,))
            accumulate(j, p_a, alpha_a, (i,))
            scores(j + 2, z_a, (i,))
        return carry

    n_trips = (qi + 2) // 2
    lax.fori_loop(1, n_trips, trip, 0)
    accumulate(2 * n_trips - 1, p_b, alpha_b)

    lam = (jnp.exp(jnp.sum(lq1_ref[...] * lk1_ref[...], axis=1, keepdims=True))
           - jnp.exp(jnp.sum(lq2_ref[...] * lk2_ref[...], axis=1, keepdims=True))
           + lam_init)
    for g in range(DIFF_GROUPS):
        normed = []
        for i in (2 * g, 2 * g + 1):
            normed.append(acc_ref[i, :LANES, :] * (1.0 / acc_ref[i, LANES:LANES + 1, :]))
        yt = normed[0] - lam * normed[1]
        ms = jnp.mean(yt * yt, axis=0, keepdims=True)
        yt = yt * lax.rsqrt(ms + NORM_EPS) * g_ref[...] * (1.0 - lam_init)
        o_ref[0, :, g * LANES:(g + 1) * LANES] = yt.T.astype(BF16)


def _diff_attention(qt, qkv, vt, bias_tiles, lq1, lk1, lq2, lk2, g_subln, b, s, lam_init):
    n_blk = DIFF_HEADS // DIFF_GROUPS
    first = 3 * SB_HEADS * HEAD_DIM // (DIFF_GROUPS * LANES)
    width = DIFF_GROUPS * LANES
    n_chains = 2 * DIFF_GROUPS
    vec = pl.BlockSpec((1, HEAD_DIM), lambda bi, h, qi: (0, 0))
    return pl.pallas_call(
        functools.partial(_diff_kernel, lam_init=lam_init),
        grid=(b, n_blk, s // TQ),
        in_specs=[
            pl.BlockSpec((1, 1, width, TQ), lambda bi, h, qi: (bi, qi, h, 0)),
            pl.BlockSpec((1, s, width), lambda bi, h, qi: (bi, 0, first + n_blk + h)),
            pl.BlockSpec((1, s // TK, DIFF_GROUPS * VT_ROWS, TK), lambda bi, h, qi: (bi, 0, h, 0)),
            pl.BlockSpec((DIFF_GROUPS, 4, TK, TQ), lambda bi, h, qi: (h, 0, 0, 0)),
            vec, vec, vec, vec,
            pl.BlockSpec((LANES, 1), lambda bi, h, qi: (0, 0)),
        ],
        out_specs=pl.BlockSpec((1, TQ, width), lambda bi, h, qi: (bi, qi, h)),
        out_shape=jax.ShapeDtypeStruct((b, s, DIFF_HEADS * LANES), BF16),
        scratch_shapes=[
            pltpu.VMEM((n_chains, TK, TQ), F32), pltpu.VMEM((n_chains, TK, TQ), F32),
            pltpu.VMEM((n_chains, TK, TQ), BF16), pltpu.VMEM((n_chains, TK, TQ), BF16),
            pltpu.VMEM((n_chains, 1, TQ), F32),
            pltpu.VMEM((n_chains, 1, TQ), F32), pltpu.VMEM((n_chains, 1, TQ), F32),
            pltpu.VMEM((n_chains, VT_ROWS, TQ), F32),
        ],
        compiler_params=pltpu.CompilerParams(
            dimension_semantics=("parallel", "parallel", "arbitrary"),
            vmem_limit_bytes=VMEM_LIMIT),
        name="diff_attention",
    )(qt, qkv, vt, bias_tiles, lq1, lk1, lq2, lk2, g_subln)


def _t5_bucket(rel):
    half = REL_BUCKETS // 2
    max_exact = half // 2
    ret = jnp.where(rel > 0, half, 0)
    n = jnp.abs(rel)
    nf = jnp.maximum(n, 1).astype(F32)
    large = max_exact + (jnp.log(nf / max_exact) / math.log(REL_MAX_DIST / max_exact)
                         * (half - max_exact)).astype(jnp.int32)
    large = jnp.minimum(large, half - 1)
    return ret + jnp.where(n < max_exact, n, large)


def _toeplitz(vals, first):
    period = vals.shape[1]
    rolled = jnp.roll(vals, -first, axis=1)
    skew = jnp.tile(rolled, (1, TQ))[:, :TQ * (period - 1)].reshape(-1, TQ, period - 1)
    return skew[:, :, :TK]


def _bias_tiles(rel_bias):
    rel = jnp.arange(-2 * TQ + 1, TK + 1, dtype=jnp.int32)
    by_rel = rel_bias.astype(F32)[_t5_bucket(rel)].T * LOG2E
    key = jnp.arange(TK)[:, None]
    query = jnp.arange(TQ)[None, :]
    allowed = key < (query // CHUNK + 1) * CHUNK
    diag = jnp.where(allowed, jnp.swapaxes(_toeplitz(by_rel, 2 * TQ - 1), 1, 2), MASKED)
    near = jnp.swapaxes(_toeplitz(by_rel, 2 * TQ - 1 - TK), 1, 2)
    far = jnp.broadcast_to(by_rel[:, :1, None], near.shape)
    dead = jnp.full(near.shape, MASKED, F32)
    return jnp.stack([diag, near, far, dead], axis=1)


def _merge_kernel(x_ref, ysb_ref, ydf_ref, gpre_ref, wg_ref, wsb_ref, wdf_ref, wo_ref,
                  gpost_ref, o_ref):
    d = x_ref.shape[1]
    blocks = _row_blocks(x_ref.shape[0])
    gpre = gpre_ref[...]
    h = _rms(x_ref[blocks[0], :], gpre).astype(BF16)
    for k, rows in enumerate(blocks):
        h_now = h
        if k + 1 < len(blocks):
            h = _rms(x_ref[blocks[k + 1], :], gpre).astype(BF16)
        gates = jax.nn.sigmoid(_dot(h_now, wg_ref[...]))
        merged = (gates[:, :d] * _dot(ysb_ref[rows, :], wsb_ref[...])
                  + gates[:, d:] * _dot(ydf_ref[rows, :], wdf_ref[...]))
        o = _dot(merged.astype(BF16), wo_ref[...])
        o_ref[rows, :] = x_ref[rows, :] + _rms(o, gpost_ref[...])


def _merge(x2, ysb, ydf, gpre, wg, wsb, wdf, wo, gpost, tm):
    m, d = x2.shape

    def full(a):
        return pl.BlockSpec(a.shape, lambda i: (0, 0))

    def rows(a):
        return pl.BlockSpec((tm, a.shape[1]), lambda i: (i, 0))

    return pl.pallas_call(
        _merge_kernel,
        grid=(m // tm,),
        in_specs=[rows(x2), rows(ysb), rows(ydf), full(gpre), full(wg), full(wsb), full(wdf),
                  full(wo), full(gpost)],
        out_specs=pl.BlockSpec((tm, d), lambda i: (i, 0)),
        out_shape=jax.ShapeDtypeStruct((m, d), F32),
        compiler_params=pltpu.CompilerParams(
            dimension_semantics=("parallel",),
            vmem_limit_bytes=VMEM_LIMIT),
        name="merge",
    )(x2, ysb, ydf, gpre, wg, wsb, wdf, wo, gpost)


def _mlp_kernel(x_ref, gpre_ref, wup_ref, wdn_ref, gpost_ref, o_ref):
    blocks = _row_blocks(x_ref.shape[0])
    gpre = gpre_ref[...]
    h = _rms(x_ref[blocks[0], :], gpre).astype(BF16)
    for k, rows in enumerate(blocks):
        h_now = h
        if k + 1 < len(blocks):
            h = _rms(x_ref[blocks[k + 1], :], gpre).astype(BF16)
        acc = None
        for c in range(0, wup_ref.shape[1], COL_BLOCK):
            u = jnp.square(jnp.maximum(_dot(h_now, wup_ref[:, c:c + COL_BLOCK]), 0.0))
            part = _dot(u.astype(BF16), wdn_ref[c:c + COL_BLOCK, :])
            acc = part if acc is None else acc + part
        o_ref[rows, :] = x_ref[rows, :] + _rms(acc, gpost_ref[...])


def _mlp(x2, gpre, wup, wdn, gpost, tm):
    m, d = x2.shape

    def full(a):
        return pl.BlockSpec(a.shape, lambda i: (0, 0))

    return pl.pallas_call(
        _mlp_kernel,
        grid=(m // tm,),
        in_specs=[pl.BlockSpec((tm, d), lambda i: (i, 0)), full(gpre), full(wup), full(wdn),
                  full(gpost)],
        out_specs=pl.BlockSpec((tm, d), lambda i: (i, 0)),
        out_shape=jax.ShapeDtypeStruct((m, d), F32),
        compiler_params=pltpu.CompilerParams(
            dimension_semantics=("parallel",),
            vmem_limit_bytes=VMEM_LIMIT),
        name="mlp",
    )(x2, gpre, wup, wdn, gpost)


def kernel(x, w_in, w_sb_out, w_diff_out, w_o, lambda_q1, lambda_k1, lambda_q2, lambda_k2, w_subln, rel_bias, g_pre_mix, g_post_mix, g_pre_mlp, g_post_mlp, w_up, w_down):
    b, s, d = x.shape
    depth = w_in.shape[0]
    sb_width = SB_HEADS * HEAD_DIM
    diff_width = DIFF_HEADS * LANES
    n_qkv = 3 * sb_width + 3 * diff_width
    scale = HEAD_DIM ** -0.5
    col_scale = jnp.ones((n_qkv,), F32)
    col_scale = col_scale.at[:sb_width].set(scale)
    col_scale = col_scale.at[3 * sb_width:3 * sb_width + diff_width].set(scale * LOG2E)
    bias_tiles = _bias_tiles(rel_bias)

    x2 = x.reshape(b * s, d)
    for l in range(depth):
        lam_init = 0.8 - 0.6 * math.exp(-0.3 * l)
        w_qkv = (w_in[l][:, :n_qkv] * col_scale).astype(BF16)
        w_gate = w_in[l][:, n_qkv:].astype(BF16)
        d_q0 = 3 * sb_width
        d_v0 = d_q0 + 2 * diff_width
        qkv, q_t, v_t = _proj(x2, g_pre_mix[l][None], w_qkv, 1024, b, s,
                              (d_q0, d_q0 + diff_width), (d_v0, d_v0 + diff_width))
        qkv = qkv.reshape(b, s, n_qkv)
        y_sb = _sb_attention(qkv, b, s)
        y_diff = _diff_attention(q_t, qkv, v_t, bias_tiles, lambda_q1[l][None],
                                 lambda_k1[l][None], lambda_q2[l][None], lambda_k2[l][None],
                                 w_subln[l][:, None], b, s, lam_init)
        x2 = _merge(x2, y_sb.reshape(b * s, sb_width), y_diff.reshape(b * s, diff_width),
                    g_pre_mix[l][None], w_gate, w_sb_out[l].astype(BF16),
                    w_diff_out[l].astype(BF16), w_o[l].astype(BF16), g_post_mix[l][None], 1024)
        x2 = _mlp(x2, g_pre_mlp[l][None], w_up[l].astype(BF16), w_down[l].astype(BF16),
                  g_post_mlp[l][None], 1024)
    return x2.reshape(b, s, d)
```

```python
import functools
import math

import jax
import jax.numpy as jnp
from jax import lax
from jax.experimental import pallas as pl
from jax.experimental.pallas import tpu as pltpu

F32 = jnp.float32
BF16 = jnp.bfloat16

NORM_EPS = 1e-6
CHUNK = 64
SB_HEADS = 8
HEAD_DIM = 64
DIFF_HEADS = 4
LANES = 128
REL_BUCKETS = 32
REL_MAX_DIST = 128
MASKED = -1e30
LOG2E = 1.4426950408889634
DEAD_LOG = -111.0

VMEM_LIMIT = 56 * 1024 * 1024

TQ = 256
TK = 256
GROUPS = 4
DIFF_GROUPS = 4


def _rms(xf, g):
    ms = jnp.mean(xf * xf, axis=-1, keepdims=True)
    return xf * lax.rsqrt(ms + NORM_EPS) * g


def _dot(a, b):
    return jnp.dot(a, b, preferred_element_type=F32)


def _dot_nt(a, b):
    return lax.dot_general(a, b, (((1,), (1,)), ((), ())), preferred_element_type=F32)


def _half_lane_queries(q_ref):
    lane = lax.broadcasted_iota(jnp.int32, (TQ, LANES), 1)
    qs = []
    for g in range(GROUPS):
        q_all = q_ref[0, :, g * LANES:(g + 1) * LANES]
        for half in range(2):
            keep = (lane >= half * HEAD_DIM) & (lane < (half + 1) * HEAD_DIM)
            qs.append(jnp.where(keep, q_all, jnp.zeros_like(q_all)))
    return qs


def _key_tile(ref, kb, g):
    start = pl.multiple_of(kb * TK, TK)
    return ref[0, pl.ds(start, TK), g * LANES:(g + 1) * LANES]


ROW_BLOCK = 256
COL_BLOCK = 1024


def _row_blocks(tm):
    return [slice(r, r + ROW_BLOCK) for r in range(0, tm, ROW_BLOCK)]


def _proj_kernel(x_ref, g_ref, w_ref, o_ref, qt_ref, vt_ref, *, q_cols, v_cols):
    blocks = _row_blocks(x_ref.shape[0])
    g = g_ref[...]
    h = _rms(x_ref[blocks[0], :], g).astype(BF16)
    for k, rows in enumerate(blocks):
        h_now = h
        if k + 1 < len(blocks):
            h = _rms(x_ref[blocks[k + 1], :], g).astype(BF16)
        for c in range(0, w_ref.shape[1], COL_BLOCK):
            res = _dot(h_now, w_ref[:, c:c + COL_BLOCK])
            o_ref[rows, c:c + COL_BLOCK] = res.astype(BF16)
            if c <= q_cols[0] and q_cols[1] <= c + COL_BLOCK:
                qt_ref[0, k] = res[:, q_cols[0] - c:q_cols[1] - c].T.astype(BF16)
            if c <= v_cols[0] and v_cols[1] <= c + COL_BLOCK:
                v_t = res[:, v_cols[0] - c:v_cols[1] - c].T.astype(BF16)
                for head in range(DIFF_HEADS):
                    top = head * VT_ROWS
                    vt_ref[0, k, top:top + LANES, :] = v_t[head * LANES:(head + 1) * LANES, :]
                    vt_ref[0, k, top + LANES:top + VT_ROWS, :] = jnp.ones((ONES_ROWS, TK), BF16)


def _proj(x2, g, w, tm, b, s, q_cols, v_cols):
    m, d = x2.shape
    n = w.shape[1]
    per_batch = s // tm
    tiles = tm // ROW_BLOCK
    return pl.pallas_call(
        functools.partial(_proj_kernel, q_cols=q_cols, v_cols=v_cols),
        grid=(m // tm,),
        in_specs=[
            pl.BlockSpec((tm, d), lambda i: (i, 0)),
            pl.BlockSpec((1, d), lambda i: (0, 0)),
            pl.BlockSpec((d, n), lambda i: (0, 0)),
        ],
        out_specs=[
            pl.BlockSpec((tm, n), lambda i: (i, 0)),
            pl.BlockSpec((1, tiles, q_cols[1] - q_cols[0], TQ),
                         lambda i: (i // per_batch, i % per_batch, 0, 0)),
            pl.BlockSpec((1, tiles, DIFF_HEADS * VT_ROWS, TK),
                         lambda i: (i // per_batch, i % per_batch, 0, 0)),
        ],
        out_shape=[
            jax.ShapeDtypeStruct((m, n), BF16),
            jax.ShapeDtypeStruct((b, s // TQ, q_cols[1] - q_cols[0], TQ), BF16),
            jax.ShapeDtypeStruct((b, s // TK, DIFF_HEADS * VT_ROWS, TK), BF16),
        ],
        compiler_params=pltpu.CompilerParams(
            dimension_semantics=("parallel",),
            vmem_limit_bytes=VMEM_LIMIT),
        name="proj",
    )(x2, g, w)


def _sb_scores(qs, ks):
    return [_dot_nt(q, k) for q, k in zip(qs, ks)]


def _sb_weights(zs, vs, upper, states, mask):
    n = len(zs)
    log_betas, log_oms, his, los = [], [], [], []
    for z in zs:
        soft = jnp.log(1.0 + jnp.exp2(jnp.abs(z) * -LOG2E))
        log_beta = jnp.minimum(z, 0.0) - soft
        log_om = log_beta - z
        if mask is not None:
            log_om = jnp.where(mask, log_om, 0.0)
        hi = log_om.astype(BF16)
        log_betas.append(log_beta)
        log_oms.append(log_om)
        his.append(hi)
        los.append((log_om - hi.astype(F32)).astype(BF16))
    tails = [_dot(his[i], upper) + _dot(los[i], upper) for i in range(n)]
    ws = []
    for i in range(n):
        w = jnp.exp(log_betas[i] + tails[i])
        if mask is not None:
            w = jnp.where(mask, w, 0.0)
        ws.append(w.astype(BF16))
    new = []
    for i in range(n):
        c, acc = states[i]
        acc = acc + jnp.exp(c) * _dot(ws[i], vs[i])
        c = c + jnp.sum(log_oms[i], axis=1, keepdims=True)
        new.append((c, acc))
    return tuple(new)


def _sb_kernel(q_ref, k_ref, v_ref, o_ref):
    qi = pl.program_id(2)
    row = lax.broadcasted_iota(jnp.int32, (TQ, TK), 0)
    col = lax.broadcasted_iota(jnp.int32, (TQ, TK), 1)
    causal = col < row
    upper = jnp.where(row > col, 1.0, 0.0).astype(BF16)
    qs = _half_lane_queries(q_ref)
    n_chains = 2 * GROUPS

    def scores(kb):
        return _sb_scores(qs, [_key_tile(k_ref, kb, i // 2) for i in range(n_chains)])

    def weights(kb, zs, states, mask):
        vs = [_key_tile(v_ref, kb, i // 2) for i in range(n_chains)]
        return _sb_weights(zs, vs, upper, states, mask)

    def any_weight_left(states):
        c_max = functools.reduce(jnp.maximum, [jnp.max(st[0]) for st in states])
        return c_max > DEAD_LOG

    def first_two(states):
        z_diag, z_next = scores(qi), scores(qi - 1)
        states = weights(qi, z_diag, states, causal)
        return weights(qi - 1, z_next, states, None)

    def step(carry):
        j, _, states = carry
        states = weights(qi - 1 - j, scores(qi - 1 - j), states, None)
        return j + 1, any_weight_left(states), states

    init = tuple((jnp.zeros((TQ, 1), F32), jnp.zeros((TQ, LANES), F32))
                 for _ in range(n_chains))
    states = lax.cond(qi >= 1, first_two,
                      lambda st: weights(qi, scores(qi), st, causal), init)
    _, _, states = lax.while_loop(lambda carry: (carry[0] < qi) & carry[1], step,
                                  (jnp.int32(1), any_weight_left(states), states))

    lane = lax.broadcasted_iota(jnp.int32, (TQ, LANES), 1)
    for g in range(GROUPS):
        out = jnp.where(lane < HEAD_DIM, states[2 * g][1], states[2 * g + 1][1])
        o_ref[0, :, g * LANES:(g + 1) * LANES] = out.astype(BF16)


def _sb_attention(qkv, b, s):
    n_blk = SB_HEADS * HEAD_DIM // (GROUPS * LANES)
    width = GROUPS * LANES
    return pl.pallas_call(
        _sb_kernel,
        grid=(b, n_blk, s // TQ),
        in_specs=[
            pl.BlockSpec((1, TQ, width), lambda bi, p, qi: (bi, qi, p)),
            pl.BlockSpec((1, s, width), lambda bi, p, qi: (bi, 0, n_blk + p)),
            pl.BlockSpec((1, s, width), lambda bi, p, qi: (bi, 0, 2 * n_blk + p)),
        ],
        out_specs=pl.BlockSpec((1, TQ, width), lambda bi, p, qi: (bi, qi, p)),
        out_shape=jax.ShapeDtypeStruct((b, s, SB_HEADS * HEAD_DIM), BF16),
        compiler_params=pltpu.CompilerParams(
            dimension_semantics=("parallel", "parallel", "arbitrary"),
            vmem_limit_bytes=VMEM_LIMIT),
        name="sb_attention",
    )(qkv, qkv, qkv)


BIAS_DIAG, BIAS_NEAR, BIAS_FAR, BIAS_DEAD = 0, 1, 2, 3
ONES_ROWS = 16
VT_ROWS = LANES + ONES_ROWS


def _diff_kernel(qt_ref, k_ref, vt_ref, bias_ref, lq1_ref, lk1_ref, lq2_ref, lk2_ref, g_ref,
                 o_ref, z_a, z_b, p_a, p_b, m_ref, alpha_a, alpha_b, acc_ref, *, lam_init):
    qi = pl.program_id(2)
    n_chains = 2 * DIFF_GROUPS
    feature = lax.broadcasted_iota(jnp.int32, (LANES, TQ), 0)
    qts = []
    for g in range(DIFF_GROUPS):
        qt_all = qt_ref[0, 0, g * LANES:(g + 1) * LANES, :]
        for half in range(2):
            keep = (feature >= half * HEAD_DIM) & (feature < (half + 1) * HEAD_DIM)
            qts.append(jnp.where(keep, qt_all, jnp.zeros_like(qt_all)))

    def key_block(j):
        return jnp.clip(qi - j, 0, qi)

    def bias_kind(j):
        return jnp.where(j > qi, BIAS_DEAD, jnp.minimum(j, BIAS_FAR))

    def scores(j, z_ref, chains=range(2 * DIFF_GROUPS)):
        kb = key_block(j)
        for i in chains:
            z_ref[i] = _dot(_key_tile(k_ref, kb, i // 2), qts[i])

    def normalise(j, z_ref, p_ref, alpha_ref, per_element_bias, chains=range(2 * DIFF_GROUPS)):
        kind = bias_kind(j)
        for i in chains:
            m_old = m_ref[i]
            if per_element_bias:
                bias = bias_ref.at[i // 2, kind]
                m_new = jnp.maximum(m_old, jnp.max(z_ref[i] + bias[...], axis=0, keepdims=True))
                m_ref[i] = m_new
                p = jnp.exp2(z_ref[i] + bias[...] - m_new)
            else:
                c = bias_ref[i // 2, kind, 0:1, 0:1]
                m_new = jnp.maximum(m_old, jnp.max(z_ref[i], axis=0, keepdims=True) + c)
                m_ref[i] = m_new
                p = jnp.exp2(z_ref[i] - (m_new - c))
            alpha_ref[i] = jnp.exp2(m_old - m_new)
            p_ref[i] = p.astype(BF16)

    def accumulate(j, p_ref, alpha_ref, chains=range(2 * DIFF_GROUPS)):
        kb = key_block(j)
        for i in chains:
            g = i // 2
            vt = vt_ref[0, kb, g * VT_ROWS:(g + 1) * VT_ROWS, :]
            acc_ref[i] = alpha_ref[i] * acc_ref[i] + _dot(vt, p_ref[i])

    m_ref[...] = jnp.full(m_ref.shape, MASKED, F32)
    acc_ref[...] = jnp.zeros(acc_ref.shape, F32)
    scores(0, z_a)
    normalise(0, z_a, p_a, alpha_a, True)
    scores(1, z_b)
    accumulate(0, p_a, alpha_a)
    normalise(1, z_b, p_b, alpha_b, True)
    scores(2, z_a)

    def trip(t, carry):
        j = 2 * t
        for i in range(n_chains):
            normalise(j, z_a, p_a, alpha_a, False, (i,))
            accumulate(j - 1, p_b, alpha_b, (i,))
            scores(j + 1, z_b, (i,))
        for i in range(n_chains):
            normalise(j + 1, z_b, p_b, alpha_b, False, (i,))
            accumulate(j, p_a, alpha_a, (i,))
            scores(j + 2, z_a, (i,))
        return carry

    n_far = jnp.maximum(qi - 1, 0)
    n_trips = 1 + n_far // 2
    lax.fori_loop(1, n_trips, trip, 0)
    last = 2 * n_trips

    @pl.when(n_far % 2 == 1)
    def _():
        for i in range(n_chains):
            normalise(last, z_a, p_a, alpha_a, False, (i,))
            accumulate(last - 1, p_b, alpha_b, (i,))
        accumulate(last, p_a, alpha_a)

    @pl.when(n_far % 2 == 0)
    def _():
        accumulate(last - 1, p_b, alpha_b)

    lam = (jnp.exp(jnp.sum(lq1_ref[...] * lk1_ref[...], axis=1, keepdims=True))
           - jnp.exp(jnp.sum(lq2_ref[...] * lk2_ref[...], axis=1, keepdims=True))
           + lam_init)
    for g in range(DIFF_GROUPS):
        normed = []
        for i in (2 * g, 2 * g + 1):
            normed.append(acc_ref[i, :LANES, :] * (1.0 / acc_ref[i, LANES:LANES + 1, :]))
        yt = normed[0] - lam * normed[1]
        ms = jnp.mean(yt * yt, axis=0, keepdims=True)
        yt = yt * lax.rsqrt(ms + NORM_EPS) * g_ref[...] * (1.0 - lam_init)
        o_ref[0, :, g * LANES:(g + 1) * LANES] = yt.T.astype(BF16)


def _diff_attention(qt, qkv, vt, bias_tiles, lq1, lk1, lq2, lk2, g_subln, b, s, lam_init):
    n_blk = DIFF_HEADS // DIFF_GROUPS
    first = 3 * SB_HEADS * HEAD_DIM // (DIFF_GROUPS * LANES)
    width = DIFF_GROUPS * LANES
    n_chains = 2 * DIFF_GROUPS
    vec = pl.BlockSpec((1, HEAD_DIM), lambda bi, h, qi: (0, 0))
    return pl.pallas_call(
        functools.partial(_diff_kernel, lam_init=lam_init),
        grid=(b, n_blk, s // TQ),
        in_specs=[
            pl.BlockSpec((1, 1, width, TQ), lambda bi, h, qi: (bi, qi, h, 0)),
            pl.BlockSpec((1, s, width), lambda bi, h, qi: (bi, 0, first + n_blk + h)),
            pl.BlockSpec((1, s // TK, DIFF_GROUPS * VT_ROWS, TK), lambda bi, h, qi: (bi, 0, h, 0)),
            pl.BlockSpec((DIFF_GROUPS, 4, TK, TQ), lambda bi, h, qi: (h, 0, 0, 0)),
            vec, vec, vec, vec,
            pl.BlockSpec((LANES, 1), lambda bi, h, qi: (0, 0)),
        ],
        out_specs=pl.BlockSpec((1, TQ, width), lambda bi, h, qi: (bi, qi, h)),
        out_shape=jax.ShapeDtypeStruct((b, s, DIFF_HEADS * LANES), BF16),
        scratch_shapes=[
            pltpu.VMEM((n_chains, TK, TQ), F32), pltpu.VMEM((n_chains, TK, TQ), F32),
            pltpu.VMEM((n_chains, TK, TQ), BF16), pltpu.VMEM((n_chains, TK, TQ), BF16),
            pltpu.VMEM((n_chains, 1, TQ), F32),
            pltpu.VMEM((n_chains, 1, TQ), F32), pltpu.VMEM((n_chains, 1, TQ), F32),
            pltpu.VMEM((n_chains, VT_ROWS, TQ), F32),
        ],
        compiler_params=pltpu.CompilerParams(
            dimension_semantics=("parallel", "parallel", "arbitrary"),
            vmem_limit_bytes=VMEM_LIMIT),
        name="diff_attention",
    )(qt, qkv, vt, bias_tiles, lq1, lk1, lq2, lk2, g_subln)


def _t5_bucket(rel):
    half = REL_BUCKETS // 2
    max_exact = half // 2
    ret = jnp.where(rel > 0, half, 0)
    n = jnp.abs(rel)
    nf = jnp.maximum(n, 1).astype(F32)
    large = max_exact + (jnp.log(nf / max_exact) / math.log(REL_MAX_DIST / max_exact)
                         * (half - max_exact)).astype(jnp.int32)
    large = jnp.minimum(large, half - 1)
    return ret + jnp.where(n < max_exact, n, large)


def _toeplitz(vals, first):
    period = vals.shape[1]
    rolled = jnp.roll(vals, -first, axis=1)
    skew = jnp.tile(rolled, (1, TQ))[:, :TQ * (period - 1)].reshape(-1, TQ, period - 1)
    return skew[:, :, :TK]


def _bias_tiles(rel_bias):
    rel = jnp.arange(-2 * TQ + 1, TK + 1, dtype=jnp.int32)
    by_rel = rel_bias.astype(F32)[_t5_bucket(rel)].T * LOG2E
    key = jnp.arange(TK)[:, None]
    query = jnp.arange(TQ)[None, :]
    allowed = key < (query // CHUNK + 1) * CHUNK
    diag = jnp.where(allowed, jnp.swapaxes(_toeplitz(by_rel, 2 * TQ - 1), 1, 2), MASKED)
    near = jnp.swapaxes(_toeplitz(by_rel, 2 * TQ - 1 - TK), 1, 2)
    far = jnp.broadcast_to(by_rel[:, :1, None], near.shape)
    dead = jnp.full(near.shape, MASKED, F32)
    return jnp.stack([diag, near, far, dead], axis=1)


def _merge_kernel(x_ref, ysb_ref, ydf_ref, gpre_ref, wg_ref, wsb_ref, wdf_ref, wo_ref,
                  gpost_ref, o_ref):
    d = x_ref.shape[1]
    blocks = _row_blocks(x_ref.shape[0])
    gpre = gpre_ref[...]
    h = _rms(x_ref[blocks[0], :], gpre).astype(BF16)
    for k, rows in enumerate(blocks):
        h_now = h
        if k + 1 < len(blocks):
            h = _rms(x_ref[blocks[k + 1], :], gpre).astype(BF16)
        gates = jax.nn.sigmoid(_dot(h_now, wg_ref[...]))
        merged = (gates[:, :d] * _dot(ysb_ref[rows, :], wsb_ref[...])
                  + gates[:, d:] * _dot(ydf_ref[rows, :], wdf_ref[...]))
        o = _dot(merged.astype(BF16), wo_ref[...])
        o_ref[rows, :] = x_ref[rows, :] + _rms(o, gpost_ref[...])


def _merge(x2, ysb, ydf, gpre, wg, wsb, wdf, wo, gpost, tm):
    m, d = x2.shape

    def full(a):
        return pl.BlockSpec(a.shape, lambda i: (0, 0))

    def rows(a):
        return pl.BlockSpec((tm, a.shape[1]), lambda i: (i, 0))

    return pl.pallas_call(
        _merge_kernel,
        grid=(m // tm,),
        in_specs=[rows(x2), rows(ysb), rows(ydf), full(gpre), full(wg), full(wsb), full(wdf),
                  full(wo), full(gpost)],
        out_specs=pl.BlockSpec((tm, d), lambda i: (i, 0)),
        out_shape=jax.ShapeDtypeStruct((m, d), F32),
        compiler_params=pltpu.CompilerParams(
            dimension_semantics=("parallel",),
            vmem_limit_bytes=VMEM_LIMIT),
        name="merge",
    )(x2, ysb, ydf, gpre, wg, wsb, wdf, wo, gpost)


def _mlp_kernel(x_ref, gpre_ref, wup_ref, wdn_ref, gpost_ref, o_ref):
    blocks = _row_blocks(x_ref.shape[0])
    gpre = gpre_ref[...]
    h = _rms(x_ref[blocks[0], :], gpre).astype(BF16)
    for k, rows in enumerate(blocks):
        h_now = h
        if k + 1 < len(blocks):
            h = _rms(x_ref[blocks[k + 1], :], gpre).astype(BF16)
        acc = None
        for c in range(0, wup_ref.shape[1], COL_BLOCK):
            u = jnp.square(jnp.maximum(_dot(h_now, wup_ref[:, c:c + COL_BLOCK]), 0.0))
            part = _dot(u.astype(BF16), wdn_ref[c:c + COL_BLOCK, :])
            acc = part if acc is None else acc + part
        o_ref[rows, :] = x_ref[rows, :] + _rms(acc, gpost_ref[...])


def _mlp(x2, gpre, wup, wdn, gpost, tm):
    m, d = x2.shape

    def full(a):
        return pl.BlockSpec(a.shape, lambda i: (0, 0))

    return pl.pallas_call(
        _mlp_kernel,
        grid=(m // tm,),
        in_specs=[pl.BlockSpec((tm, d), lambda i: (i, 0)), full(gpre), full(wup), full(wdn),
                  full(gpost)],
        out_specs=pl.BlockSpec((tm, d), lambda i: (i, 0)),
        out_shape=jax.ShapeDtypeStruct((m, d), F32),
        compiler_params=pltpu.CompilerParams(
            dimension_semantics=("parallel",),
            vmem_limit_bytes=VMEM_LIMIT),
        name="mlp",
    )(x2, gpre, wup, wdn, gpost)


def kernel(x, w_in, w_sb_out, w_diff_out, w_o, lambda_q1, lambda_k1, lambda_q2, lambda_k2, w_subln, rel_bias, g_pre_mix, g_post_mix, g_pre_mlp, g_post_mlp, w_up, w_down):
    b, s, d = x.shape
    depth = w_in.shape[0]
    sb_width = SB_HEADS * HEAD_DIM
    diff_width = DIFF_HEADS * LANES
    n_qkv = 3 * sb_width + 3 * diff_width
    scale = HEAD_DIM ** -0.5
    col_scale = jnp.ones((n_qkv,), F32)
    col_scale = col_scale.at[:sb_width].set(scale)
    col_scale = col_scale.at[3 * sb_width:3 * sb_width + diff_width].set(scale * LOG2E)
    bias_tiles = _bias_tiles(rel_bias)

    x2 = x.reshape(b * s, d)
    for l in range(depth):
        lam_init = 0.8 - 0.6 * math.exp(-0.3 * l)
        w_qkv = (w_in[l][:, :n_qkv] * col_scale).astype(BF16)
        w_gate = w_in[l][:, n_qkv:].astype(BF16)
        d_q0 = 3 * sb_width
        d_v0 = d_q0 + 2 * diff_width
        qkv, q_t, v_t = _proj(x2, g_pre_mix[l][None], w_qkv, 1024, b, s,
                              (d_q0, d_q0 + diff_width), (d_v0, d_v0 + diff_width))
        qkv = qkv.reshape(b, s, n_qkv)
        y_sb = _sb_attention(qkv, b, s)
        y_diff = _diff_attention(q_t, qkv, v_t, bias_tiles, lambda_q1[l][None],
                                 lambda_k1[l][None], lambda_q2[l][None], lambda_k2[l][None],
                                 w_subln[l][:, None], b, s, lam_init)
        x2 = _merge(x2, y_sb.reshape(b * s, sb_width), y_diff.reshape(b * s, diff_width),
                    g_pre_mix[l][None], w_gate, w_sb_out[l].astype(BF16),
                    w_diff_out[l].astype(BF16), w_o[l].astype(BF16), g_post_mix[l][None], 1024)
        x2 = _mlp(x2, g_pre_mlp[l][None], w_up[l].astype(BF16), w_down[l].astype(BF16),
                  g_post_mlp[l][None], 1024)
    return x2.reshape(b, s, d)
```
